```python
import jax, jax.numpy as jnp
from jax import lax
import numpy as np

D_MODEL = 4096
BATCH = 4
SEQ = 4096
DEPTH = 2

CHUNK = 64
Q_BLOCK = 128
RET_HEADS = 8
RET_DK = 128
RET_DV = 256
FOX_HEADS = 8
FOX_DH = 128
MLA_HEADS = 8
MLA_Q_LORA = 1024
MLA_KV_LORA = 512
MLA_NOPE = 128
MLA_ROPE = 64
MLA_DV = 128
D_FF = 2 * D_MODEL
N_BRANCH = 3
ROPE_BASE = 10000.0
EPS = 1e-6
IN_WIDTH = (2 * RET_HEADS * RET_DK + 2 * RET_HEADS * RET_DV
            + 3 * FOX_HEADS * FOX_DH + FOX_HEADS
            + MLA_Q_LORA + MLA_KV_LORA + MLA_ROPE)

kernel_name = "hybrid_retention_fox_mla_macaron"


def rms_norm(x, g):
    xf = x.astype(jnp.float32)
    y = xf * lax.rsqrt(jnp.mean(xf * xf, axis=-1, keepdims=True) + EPS)
    return (y * g.astype(jnp.float32)).astype(x.dtype)


def swiglu(u, w13, w2):
    a, b = jnp.split(u @ w13, 2, axis=-1)
    return (jax.nn.silu(a) * b) @ w2


def rope(t, pos):
    half = t.shape[-1] // 2
    inv = ROPE_BASE ** (-jnp.arange(half, dtype=jnp.float32) / half)
    ang = pos.astype(jnp.float32)[..., None] * inv
    cos = jnp.cos(ang)[:, :, None, :]
    sin = jnp.sin(ang)[:, :, None, :]
    t1 = t[..., :half].astype(jnp.float32)
    t2 = t[..., half:].astype(jnp.float32)
    return jnp.concatenate([t1 * cos - t2 * sin, t2 * cos + t1 * sin], axis=-1).astype(t.dtype)


def retention(q, k, v):
    b_, s_, h_, dk = q.shape
    n_chunks = s_ // CHUNK
    log_g = jnp.log1p(-jnp.exp2(-5.0 - jnp.arange(h_, dtype=jnp.float32)))
    idx = jnp.arange(CHUNK, dtype=jnp.float32)
    d_intra = jnp.exp(log_g[:, None, None] * jnp.abs(idx[:, None] - idx[None, :]))
    xi = jnp.exp(log_g[:, None] * (idx + 1.0))[None, :, :, None]
    zeta = jnp.exp(log_g[:, None] * (CHUNK - 1.0 - idx))[None, :, :, None]
    g_chunk = jnp.exp(log_g * CHUNK)[None, :, None, None]

    def to_chunks(t):
        return t.astype(jnp.float32).reshape(b_, n_chunks, CHUNK, h_, -1).transpose(1, 0, 3, 2, 4)

    qc = to_chunks(q) * (dk ** -0.5)
    kc = to_chunks(k)
    vc = to_chunks(v)

    def step(state, inp):
        qi, ki, vi = inp
        scores = jnp.einsum('bhid,bhjd->bhij', qi, ki) * d_intra
        out = (jnp.einsum('bhij,bhje->bhie', scores, vi)
               + jnp.einsum('bhid,bhde->bhie', qi * xi, state))
        state = state * g_chunk + jnp.einsum('bhjd,bhje->bhde', ki * zeta, vi)
        return state, out

    state0 = jnp.zeros((b_, h_, dk, v.shape[-1]), jnp.float32)
    _, o = lax.scan(step, state0, (qc, kc, vc))
    return o.transpose(1, 0, 3, 2, 4).reshape(b_, s_, h_, -1)


def causal_mask(tq, tk):
    return tk <= tq


def chunk_mask(tq, tk):
    return (tk // CHUNK) <= (tq // CHUNK)


def blocked_attention(q, k, v, scale, mask_fn, f_cum=None):
    s_ = q.shape[1]
    outs = []
    for i in range(s_ // Q_BLOCK):
        lo, hi = i * Q_BLOCK, (i + 1) * Q_BLOCK
        s = jnp.einsum('bqhd,bkhd->bhqk', q[:, lo:hi].astype(jnp.float32),
                       k[:, :hi].astype(jnp.float32)) * scale
        if f_cum is not None:
            fq = jnp.transpose(f_cum[:, lo:hi], (0, 2, 1))[:, :, :, None]
            fk = jnp.transpose(f_cum[:, :hi], (0, 2, 1))[:, :, None, :]
            s = s + (fq - fk)
        tq = lo + jnp.arange(Q_BLOCK)
        tk = jnp.arange(hi)
        s = jnp.where(mask_fn(tq[:, None], tk[None, :]), s, jnp.finfo(jnp.float32).min)
        p = jax.nn.softmax(s, axis=-1)
        outs.append(jnp.einsum('bhqk,bkhd->bqhd', p, v[:, :hi].astype(jnp.float32)))
    return jnp.concatenate(outs, axis=1)


def hybrid_mixer(u, positions, w_in, b_forget, ret_norm, mla_q_norm, mla_kv_norm,
                 w_uq, w_ukv, w_up_ret, w_up_fox, w_up_mla, w_gate, b_gate, w_out):
    b_, s_, _ = u.shape
    widths = [RET_HEADS * RET_DK, RET_HEADS * RET_DK, RET_HEADS * RET_DV, RET_HEADS * RET_DV,
              FOX_HEADS * FOX_DH, FOX_HEADS * FOX_DH, FOX_HEADS * FOX_DH, FOX_HEADS,
              MLA_Q_LORA, MLA_KV_LORA, MLA_ROPE]
    splits = np.cumsum(widths)[:-1].tolist()
    z = u @ w_in
    rq, rk, rv, rg, fq, fk, fv, ff, cq, ckv, kr = jnp.split(z, splits, axis=-1)

    rq = rope(rq.reshape(b_, s_, RET_HEADS, RET_DK), positions)
    rk = rope(rk.reshape(b_, s_, RET_HEADS, RET_DK), positions)
    ro = retention(rq, rk, rv.reshape(b_, s_, RET_HEADS, RET_DV))
    ro = ro * lax.rsqrt(jnp.mean(ro * ro, axis=-1, keepdims=True) + EPS)
    ro = ro * ret_norm.astype(jnp.float32).reshape(RET_HEADS, RET_DV)
    ro = ro * jax.nn.silu(rg.astype(jnp.float32).reshape(b_, s_, RET_HEADS, RET_DV))
    ro = ro.reshape(b_, s_, RET_HEADS * RET_DV).astype(u.dtype)

    log_f = jax.nn.log_sigmoid(ff.astype(jnp.float32) + b_forget.astype(jnp.float32))
    f_cum = jnp.cumsum(log_f, axis=1)
    fo = blocked_attention(fq.reshape(b_, s_, FOX_HEADS, FOX_DH),
                           fk.reshape(b_, s_, FOX_HEADS, FOX_DH),
                           fv.reshape(b_, s_, FOX_HEADS, FOX_DH),
                           FOX_DH ** -0.5, causal_mask, f_cum)
    fo = fo.reshape(b_, s_, FOX_HEADS * FOX_DH).astype(u.dtype)

    qf = (rms_norm(cq, mla_q_norm) @ w_uq).reshape(b_, s_, MLA_HEADS, MLA_NOPE + MLA_ROPE)
    q_nope, q_pe = qf[..., :MLA_NOPE], rope(qf[..., MLA_NOPE:], positions)
    kvf = (rms_norm(ckv, mla_kv_norm) @ w_ukv).reshape(b_, s_, MLA_HEADS, MLA_NOPE + MLA_DV)
    k_nope, mv = kvf[..., :MLA_NOPE], kvf[..., MLA_NOPE:]
    k_pe = rope(kr[:, :, None, :], positions)
    mq = jnp.concatenate([q_nope, q_pe], axis=-1)
    mk = jnp.concatenate([k_nope, jnp.broadcast_to(k_pe, (b_, s_, MLA_HEADS, MLA_ROPE))], axis=-1)
    mo = blocked_attention(mq, mk, mv, (MLA_NOPE + MLA_ROPE) ** -0.5, chunk_mask)
    mo = mo.reshape(b_, s_, MLA_HEADS * MLA_DV).astype(u.dtype)

    merged = (jax.nn.sigmoid(u @ w_gate[0] + b_gate[0]) * (ro @ w_up_ret)
              + jax.nn.sigmoid(u @ w_gate[1] + b_gate[1]) * (fo @ w_up_fox)
              + jax.nn.sigmoid(u @ w_gate[2] + b_gate[2]) * (mo @ w_up_mla))
    return merged @ w_out


def setup_inputs(seed: int = 0) -> dict:
    key = jax.random.key(seed)
    ks = jax.random.split(key, 24)
    f32 = jnp.float32

    def dense(k, shape):
        return jax.random.normal(k, shape, f32) * (shape[-2] ** -0.5)

    def gain(k, shape):
        return 1.0 + 0.02 * jax.random.normal(k, shape, f32)

    x = jax.random.normal(ks[0], (BATCH, SEQ, D_MODEL), f32)
    positions = (jax.random.randint(ks[1], (BATCH, 1), 0, 4096, dtype=jnp.int32)
                 + jnp.arange(SEQ, dtype=jnp.int32)[None, :])
    return {
        "x": x,
        "positions": positions,
        "ffn1_norm": gain(ks[2], (DEPTH, D_MODEL)),
        "ffn1_w13": dense(ks[3], (DEPTH, D_MODEL, 2 * D_FF)),
        "ffn1_w2": dense(ks[4], (DEPTH, D_FF, D_MODEL)),
        "mix_norm": gain(ks[5], (DEPTH, D_MODEL)),
        "w_in": dense(ks[6], (DEPTH, D_MODEL, IN_WIDTH)),
        "b_forget": 2.0 + 0.1 * jax.random.normal(ks[7], (DEPTH, FOX_HEADS), f32),
        "ret_norm": gain(ks[8], (DEPTH, RET_HEADS * RET_DV)),
        "mla_q_norm": gain(ks[9], (DEPTH, MLA_Q_LORA)),
        "mla_kv_norm": gain(ks[10], (DEPTH, MLA_KV_LORA)),
        "w_uq": dense(ks[11], (DEPTH, MLA_Q_LORA, MLA_HEADS * (MLA_NOPE + MLA_ROPE))),
        "w_ukv": dense(ks[12], (DEPTH, MLA_KV_LORA, MLA_HEADS * (MLA_NOPE + MLA_DV))),
        "w_up_ret": dense(ks[13], (DEPTH, RET_HEADS * RET_DV, D_MODEL)),
        "w_up_fox": dense(ks[14], (DEPTH, FOX_HEADS * FOX_DH, D_MODEL)),
        "w_up_mla": dense(ks[15], (DEPTH, MLA_HEADS * MLA_DV, D_MODEL)),
        "w_gate": dense(ks[16], (DEPTH, N_BRANCH, D_MODEL, D_MODEL)),
        "b_gate": 0.02 * jax.random.normal(ks[17], (DEPTH, N_BRANCH, D_MODEL), f32),
        "w_out": dense(ks[18], (DEPTH, D_MODEL, D_MODEL)),
        "ffn2_norm": gain(ks[19], (DEPTH, D_MODEL)),
        "ffn2_w13": dense(ks[20], (DEPTH, D_MODEL, 2 * D_FF)),
        "ffn2_w2": dense(ks[21], (DEPTH, D_FF, D_MODEL)),
        "final_norm": gain(ks[22], (D_MODEL,)),
    }


def reference(x, positions, ffn1_norm, ffn1_w13, ffn1_w2, mix_norm, w_in, b_forget,
              ret_norm, mla_q_norm, mla_kv_norm, w_uq, w_ukv, w_up_ret, w_up_fox,
              w_up_mla, w_gate, b_gate, w_out, ffn2_norm, ffn2_w13, ffn2_w2, final_norm):
    h = x
    for l in range(DEPTH):
        h = h + 0.5 * swiglu(rms_norm(h, ffn1_norm[l]), ffn1_w13[l], ffn1_w2[l])
        h = h + hybrid_mixer(rms_norm(h, mix_norm[l]), positions, w_in[l], b_forget[l],
                             ret_norm[l], mla_q_norm[l], mla_kv_norm[l], w_uq[l], w_ukv[l],
                             w_up_ret[l], w_up_fox[l], w_up_mla[l], w_gate[l], b_gate[l],
                             w_out[l])
        h = h + 0.5 * swiglu(rms_norm(h, ffn2_norm[l]), ffn2_w13[l], ffn2_w2[l])
    return rms_norm(h, final_norm)
```

```python
import functools
import math

import jax
import jax.numpy as jnp
from jax import lax
from jax.experimental import pallas as pl
from jax.experimental.pallas import tpu as pltpu

F32 = jnp.float32
BF16 = jnp.bfloat16

CHUNK = 64
RET_HEADS, RET_DK, RET_DV = 8, 128, 256
FOX_HEADS, FOX_DH = 8, 128
MLA_HEADS, MLA_Q_LORA, MLA_KV_LORA = 8, 1024, 512
MLA_NOPE, MLA_ROPE, MLA_DV = 128, 64, 128
MLA_QK_PAD = 256
ROPE_BASE = 10000.0
EPS = 1e-6
NEG_BIG = -1e30

LANES = 128
SUBLANES = 8
VMEM_LIMIT_BYTES = 56 * 1024 * 1024

A_RQ, A_RK, A_CQ, A_CKV, A_KR = 0, 1024, 2048, 3072, 3584
A_FF = A_KR + MLA_ROPE
A_WIDTH = 4096
FF_LANE = A_FF - A_KR
B_RV, B_RG, B_FQ, B_FK, B_FV = 0, 2048, 4096, 5120, 6144
B_WIDTH = 7168


def _tile(dim, pref, quantum=LANES):
    if dim <= pref:
        return dim
    t = (pref // quantum) * quantum
    while t > quantum and dim % t:
        t -= quantum
    assert dim % t == 0, (dim, pref)
    return t


def _params(*sem):
    return pltpu.CompilerParams(dimension_semantics=sem, vmem_limit_bytes=VMEM_LIMIT_BYTES)


def _dot(a, b):
    return jnp.dot(a, b, preferred_element_type=F32)


def _dot_nt(a, b):
    return lax.dot_general(a, b, (((1,), (1,)), ((), ())), preferred_element_type=F32)


def _dot_tn(a, b):
    return lax.dot_general(a, b, (((0,), (0,)), ((), ())), preferred_element_type=F32)


def _rmsnorm_kernel(x_ref, g_ref, o_ref):
    x = x_ref[...]
    ms = jnp.mean(x * x, axis=-1, keepdims=True)
    o_ref[...] = (x * lax.rsqrt(ms + EPS) * g_ref[...]).astype(o_ref.dtype)


def rmsnorm(x, g, out_dtype):
    m, d = x.shape
    bm = _tile(m, 512, SUBLANES)
    return pl.pallas_call(
        _rmsnorm_kernel,
        grid=(m // bm,),
        in_specs=[pl.BlockSpec((bm, d), lambda i: (i, 0)),
                  pl.BlockSpec((1, d), lambda i: (0, 0))],
        out_specs=pl.BlockSpec((bm, d), lambda i: (i, 0)),
        out_shape=jax.ShapeDtypeStruct((m, d), out_dtype),
        compiler_params=_params("parallel"),
        name="rmsnorm",
    )(x, g.reshape(1, d))


def _mm_kernel(a_ref, w_ref, *refs, nk, scale, has_res):
    if has_res:
        r_ref, o_ref, *scratch = refs
    else:
        o_ref, *scratch = refs

    def finish(acc):
        if has_res:
            o_ref[...] = r_ref[...] + scale * acc
        else:
            o_ref[...] = acc.astype(o_ref.dtype)

    part = _dot(a_ref[...], w_ref[...])
    if nk == 1:
        finish(part)
        return
    acc_ref, = scratch
    k = pl.program_id(2)

    @pl.when(k == 0)
    def _():
        acc_ref[...] = part

    @pl.when(jnp.logical_and(k > 0, k < nk - 1))
    def _():
        acc_ref[...] += part

    @pl.when(k == nk - 1)
    def _():
        finish(acc_ref[...] + part)


def matmul(a, w, *, out_dtype=F32, res=None, scale=1.0, bm=1024, bn=1024, bk=None, name="matmul"):
    m, kd = a.shape
    n = w.shape[1]
    bm, bn = _tile(m, bm, SUBLANES), _tile(n, bn)
    bk = kd if bk is None else _tile(kd, bk)
    nk = kd // bk
    in_specs = [pl.BlockSpec((bm, bk), lambda i, j, k: (i, k)),
                pl.BlockSpec((bk, bn), lambda i, j, k: (k, j))]
    args = [a, w]
    if res is not None:
        in_specs.append(pl.BlockSpec((bm, bn), lambda i, j, k: (i, j)))
        args.append(res)
    return pl.pallas_call(
        functools.partial(_mm_kernel, nk=nk, scale=scale, has_res=res is not None),
        grid=(m // bm, n // bn, nk),
        in_specs=in_specs,
        out_specs=pl.BlockSpec((bm, bn), lambda i, j, k: (i, j)),
        out_shape=jax.ShapeDtypeStruct((m, n), out_dtype),
        scratch_shapes=[pltpu.VMEM((bm, bn), F32)] if nk > 1 else [],
        compiler_params=_params("parallel", "parallel", "arbitrary"),
        name=name,
    )(*args)


def _swiglu_kernel(u_ref, w1_ref, w3_ref, o_ref):
    u = u_ref[...]
    a = _dot(u, w1_ref[...])
    b = _dot(u, w3_ref[...])
    o_ref[...] = (a * jax.nn.sigmoid(a) * b).astype(o_ref.dtype)


def swiglu_up(u, w13, *, bm=1024, bn=512):
    m, d = u.shape
    f = w13.shape[1] // 2
    bm, bn = _tile(m, bm, SUBLANES), _tile(f, bn)
    nf = f // bn
    return pl.pallas_call(
        _swiglu_kernel,
        grid=(m // bm, nf),
        in_specs=[pl.BlockSpec((bm, d), lambda i, j: (i, 0)),
                  pl.BlockSpec((d, bn), lambda i, j: (0, j)),
                  pl.BlockSpec((d, bn), lambda i, j: (0, j + nf))],
        out_specs=pl.BlockSpec((bm, bn), lambda i, j: (i, j)),
        out_shape=jax.ShapeDtypeStruct((m, f), BF16),
        compiler_params=_params("parallel", "parallel"),
        name="swiglu_up",
    )(u, w13, w13)


def _merge_kernel(u_ref, ro_ref, fo_ref, mo_ref, wg_ref, bg_ref, wr_ref, wf_ref, wm_ref, o_ref):
    u = u_ref[...]
    total = None
    for i, (x_ref, w_ref) in enumerate(((ro_ref, wr_ref), (fo_ref, wf_ref), (mo_ref, wm_ref))):
        gate = jax.nn.sigmoid(_dot(u, wg_ref[i]) + bg_ref[i:i + 1, :])
        term = gate * _dot(x_ref[...], w_ref[...])
        total = term if total is None else total + term
    o_ref[...] = total.astype(o_ref.dtype)


def gated_merge(u, ro, fo, mo, w_gate, b_gate, w_up_ret, w_up_fox, w_up_mla, *, bm=512, bn=256):
    m, d = u.shape
    n = w_gate.shape[-1]
    bm, bn = _tile(m, bm, SUBLANES), _tile(n, bn)
    row = lambda width: pl.BlockSpec((bm, width), lambda i, j: (i, 0))
    col = lambda depth: pl.BlockSpec((depth, bn), lambda i, j: (0, j))
    return pl.pallas_call(
        _merge_kernel,
        grid=(m // bm, n // bn),
        in_specs=[row(d), row(ro.shape[1]), row(fo.shape[1]), row(mo.shape[1]),
                  pl.BlockSpec((3, d, bn), lambda i, j: (0, 0, j)),
                  pl.BlockSpec((3, bn), lambda i, j: (0, j)),
                  col(ro.shape[1]), col(fo.shape[1]), col(mo.shape[1])],
        out_specs=pl.BlockSpec((bm, bn), lambda i, j: (i, j)),
        out_shape=jax.ShapeDtypeStruct((m, n), BF16),
        compiler_params=_params("parallel", "parallel"),
        name="gated_merge",
    )(u, ro, fo, mo, w_gate, b_gate, w_up_ret, w_up_fox, w_up_mla)


def _rope_table_kernel(pos_ref, inv_ref, coef_ref, rc_ref, rs_ref, mc_ref, ma_ref, mb_ref):
    p = pos_ref[...].astype(F32)
    ang_r = p * inv_ref[0:1, :]
    ang_m = p * inv_ref[1:2, :]
    rc_ref[...] = jnp.cos(ang_r)
    rs_ref[...] = jnp.sin(ang_r) * coef_ref[0:1, :]
    sin_m = jnp.sin(ang_m)
    mc_ref[...] = jnp.cos(ang_m) * coef_ref[1:2, :]
    ma_ref[...] = sin_m * coef_ref[2:3, :]
    mb_ref[...] = sin_m * coef_ref[3:4, :]


def rope_tables(positions):
    t = positions.size
    h_r, h_m = RET_DK // 2, MLA_ROPE // 2
    inv_r = ROPE_BASE ** (-jnp.arange(h_r, dtype=F32) / h_r)
    inv_m = ROPE_BASE ** (-jnp.arange(h_m, dtype=F32) / h_m)
    zeros_m = jnp.zeros((LANES - 2 * h_m,), F32)
    ones_m = jnp.ones((h_m,), F32)
    inv = jnp.stack([jnp.concatenate([inv_r, inv_r]),
                     jnp.concatenate([inv_m, inv_m, zeros_m])])
    coef = jnp.stack([
        jnp.concatenate([-jnp.ones((h_r,), F32), jnp.ones((h_r,), F32)]),
        jnp.concatenate([ones_m, ones_m, zeros_m]),
        jnp.concatenate([-ones_m, 0 * ones_m, zeros_m]),
        jnp.concatenate([0 * ones_m, ones_m, zeros_m]),
    ])
    bm = _tile(t, 1024, SUBLANES)
    tab = jax.ShapeDtypeStruct((t, LANES), F32)
    return pl.pallas_call(
        _rope_table_kernel,
        grid=(t // bm,),
        in_specs=[pl.BlockSpec((bm, 1), lambda i: (i, 0)),
                  pl.BlockSpec((2, LANES), lambda i: (0, 0)),
                  pl.BlockSpec((4, LANES), lambda i: (0, 0))],
        out_specs=[pl.BlockSpec((bm, LANES), lambda i: (i, 0))] * 5,
        out_shape=[tab] * 5,
        compiler_params=_params("parallel"),
        name="rope_tables",
    )(positions.reshape(t, 1), inv, coef)


def _ret_log_gamma(h):
    return math.log1p(-2.0 ** (-5.0 - h))


def _retention_kernel(q_ref, k_ref, v_ref, g_ref, rc_ref, rs_ref, norm_ref, o_ref, state_ref, decay_ref,
                      *, blk):
    first = jnp.logical_and(pl.program_id(0) == 0, pl.program_id(1) == 0)

    @pl.when(first)
    def _():
        row = lax.broadcasted_iota(jnp.int32, (blk, blk), 0)
        col = lax.broadcasted_iota(jnp.int32, (blk, blk), 1)
        dist = jnp.abs(row - col).astype(F32)
        visible = (col // CHUNK) <= (row // CHUNK)
        for h in range(RET_HEADS):
            decay_ref[h] = jnp.where(visible, jnp.exp(_ret_log_gamma(h) * dist), 0.0)

    @pl.when(pl.program_id(1) == 0)
    def _():
        state_ref[...] = jnp.zeros_like(state_ref)

    rc = rc_ref[0]
    rs = rs_ref[0]
    idx = lax.broadcasted_iota(jnp.int32, (blk, 1), 0).astype(F32)
    for h in range(RET_HEADS):
        lg = _ret_log_gamma(h)
        qk = slice(h * RET_DK, (h + 1) * RET_DK)
        vv = slice(h * RET_DV, (h + 1) * RET_DV)
        q = q_ref[0, :, qk]
        k = k_ref[0, :, qk]
        q = (q * rc + pltpu.roll(q, RET_DK // 2, 1) * rs) * (RET_DK ** -0.5)
        k = k * rc + pltpu.roll(k, RET_DK // 2, 1) * rs
        v = v_ref[0, :, vv]
        state = state_ref[h]
        scores = _dot_nt(q.astype(BF16), k.astype(BF16)) * decay_ref[h]
        q_in = (q * jnp.exp(lg * (idx + 1.0))).astype(BF16)
        out = _dot(scores.astype(BF16), v) + _dot(q_in, state.astype(BF16))
        k_out = (k * jnp.exp(lg * (blk - 1.0 - idx))).astype(BF16)
        state_ref[h] = state * math.exp(lg * blk) + _dot_tn(k_out, v)
        out = out * lax.rsqrt(jnp.mean(out * out, axis=-1, keepdims=True) + EPS)
        gate = g_ref[0, :, vv].astype(F32)
        out = out * norm_ref[:, vv] * (gate * jax.nn.sigmoid(gate))
        o_ref[0, :, vv] = out.astype(o_ref.dtype)


def retention(za, zb, rc, rs, ret_norm, *, blk=256):
    b, s, _ = za.shape
    blk = _tile(s, blk, CHUNK)
    qw, vw = RET_HEADS * RET_DK, RET_HEADS * RET_DV
    return pl.pallas_call(
        functools.partial(_retention_kernel, blk=blk),
        grid=(b, s // blk),
        in_specs=[pl.BlockSpec((1, blk, qw), lambda i, j: (i, j, A_RQ // qw)),
                  pl.BlockSpec((1, blk, qw), lambda i, j: (i, j, A_RK // qw)),
                  pl.BlockSpec((1, blk, vw), lambda i, j: (i, j, B_RV // vw)),
                  pl.BlockSpec((1, blk, vw), lambda i, j: (i, j, B_RG // vw)),
                  pl.BlockSpec((1, blk, LANES), lambda i, j: (i, j, 0)),
                  pl.BlockSpec((1, blk, LANES), lambda i, j: (i, j, 0)),
                  pl.BlockSpec((1, vw), lambda i, j: (0, 0))],
        out_specs=pl.BlockSpec((1, blk, vw), lambda i, j: (i, j, 0)),
        out_shape=jax.ShapeDtypeStruct((b, s, vw), BF16),
        scratch_shapes=[pltpu.VMEM((RET_HEADS, RET_DK, RET_DV), F32),
                        pltpu.VMEM((RET_HEADS, blk, blk), F32)],
        compiler_params=_params("arbitrary", "arbitrary"),
        name="retention",
    )(za, za, zb, zb, rc, rs, ret_norm.reshape(1, vw))


def _forget_cumsum_kernel(z_ref, bias_ref, fc_ref, fct_ref, carry_ref, *, rows):
    @pl.when(pl.program_id(1) == 0)
    def _():
        carry_ref[...] = jnp.zeros_like(carry_ref)

    lane = lax.broadcasted_iota(jnp.int32, (rows, LANES), 1)
    is_gate = jnp.logical_and(lane >= FF_LANE, lane < FF_LANE + FOX_HEADS)
    x = z_ref[0] + bias_ref[...]
    log_f = jnp.minimum(x, 0.0) - jnp.log(1.0 + jnp.exp(-jnp.abs(x)))
    log_f = jnp.where(is_gate, log_f, 0.0)
    tri = (lax.broadcasted_iota(jnp.int32, (rows, rows), 0)
           >= lax.broadcasted_iota(jnp.int32, (rows, rows), 1)).astype(BF16)
    hi = log_f.astype(BF16)
    rest = log_f - hi.astype(F32)
    mid = rest.astype(BF16)
    lo = (rest - mid.astype(F32)).astype(BF16)
    cum = _dot(tri, hi) + _dot(tri, mid) + _dot(tri, lo) + carry_ref[...]
    carry_ref[...] = cum[rows - 1:rows, :]
    fc_ref[0] = cum
    fct_ref[0] = cum.T


def forget_cumsum(za, b_forget, *, rows=256):
    b, s, _ = za.shape
    rows = _tile(s, rows)
    bias = jnp.zeros((1, LANES), F32).at[0, FF_LANE:FF_LANE + FOX_HEADS].set(b_forget)
    return pl.pallas_call(
        functools.partial(_forget_cumsum_kernel, rows=rows),
        grid=(b, s // rows),
        in_specs=[pl.BlockSpec((1, rows, LANES), lambda i, j: (i, j, A_KR // LANES)),
                  pl.BlockSpec((1, LANES), lambda i, j: (0, 0))],
        out_specs=[pl.BlockSpec((1, rows, LANES), lambda i, j: (i, j, 0)),
                   pl.BlockSpec((1, LANES, rows), lambda i, j: (i, 0, j))],
        out_shape=[jax.ShapeDtypeStruct((b, s, LANES), F32),
                   jax.ShapeDtypeStruct((b, LANES, s), F32)],
        scratch_shapes=[pltpu.VMEM((1, LANES), F32)],
        compiler_params=_params("arbitrary", "arbitrary"),
        name="forget_cumsum",
    )(za, bias)


def _attn_kernel(*refs, has_bias, chunked_mask, scale, tq, tk):
    if has_bias:
        q_ref, k_ref, v_ref, fc_ref, fct_ref, o_ref, m_ref, l_ref, acc_ref, fq_ref = refs
    else:
        q_ref, k_ref, v_ref, o_ref, m_ref, l_ref, acc_ref = refs
    h, qi, ki = pl.program_id(1), pl.program_id(2), pl.program_id(3)

    @pl.when(ki == 0)
    def _():
        m_ref[...] = jnp.full_like(m_ref, NEG_BIG)
        l_ref[...] = jnp.zeros_like(l_ref)
        acc_ref[...] = jnp.zeros_like(acc_ref)
        if has_bias:
            lane = lax.broadcasted_iota(jnp.int32, (tq, LANES), 1)
            fq_ref[...] = jnp.sum(jnp.where(lane == FF_LANE + h, fc_ref[0], 0.0), axis=1, keepdims=True)

    def step(diagonal):
        s = _dot_nt(q_ref[0], k_ref[0])
        if scale != 1.0:
            s = s * scale
        if has_bias:
            s = s + (fq_ref[...] - fct_ref[0, pl.ds(h, 1), :])
        if diagonal:
            row = lax.broadcasted_iota(jnp.int32, (tq, tk), 0)
            col = lax.broadcasted_iota(jnp.int32, (tq, tk), 1)
            if chunked_mask:
                keep = (col // CHUNK) <= (row // CHUNK)
            else:
                keep = col <= row
            s = jnp.where(keep, s, NEG_BIG)
        m_old = m_ref[...]
        m_new = jnp.maximum(m_old, jnp.max(s, axis=1, keepdims=True))
        alpha = jnp.exp(m_old - m_new)
        p = jnp.exp(s - m_new)
        l_ref[...] = alpha * l_ref[...] + jnp.sum(p, axis=1, keepdims=True)
        acc_ref[...] = alpha * acc_ref[...] + _dot(p.astype(BF16), v_ref[0])
        m_ref[...] = m_new

    @pl.when(ki < qi)
    def _():
        step(False)

    @pl.when(ki == qi)
    def _():
        step(True)
        o_ref[0] = (acc_ref[...] / l_ref[...]).astype(o_ref.dtype)


def attention(q_arr, q_blk, k_arr, k_blk, v_arr, v_blk, *, heads, dk, dv, bias=None, chunked_mask,
              scale=1.0, tile=512):
    b, s, _ = q_arr.shape
    t = _tile(s, tile)
    n = s // t
    in_specs = [pl.BlockSpec((1, t, dk), lambda i, h, qi, ki: (i, qi, q_blk + h)),
                pl.BlockSpec((1, t, dk), lambda i, h, qi, ki: (i, jnp.minimum(ki, qi), k_blk + h)),
                pl.BlockSpec((1, t, dv), lambda i, h, qi, ki: (i, jnp.minimum(ki, qi), v_blk + h))]
    args = [q_arr, k_arr, v_arr]
    scratch = [pltpu.VMEM((t, 1), F32), pltpu.VMEM((t, 1), F32), pltpu.VMEM((t, dv), F32)]
    if bias is not None:
        in_specs += [pl.BlockSpec((1, t, LANES), lambda i, h, qi, ki: (i, qi, 0)),
                     pl.BlockSpec((1, SUBLANES, t),
                                  lambda i, h, qi, ki: (i, FF_LANE // SUBLANES, jnp.minimum(ki, qi)))]
        args += list(bias)
        scratch.append(pltpu.VMEM((t, 1), F32))
    return pl.pallas_call(
        functools.partial(_attn_kernel, has_bias=bias is not None, chunked_mask=chunked_mask, scale=scale,
                          tq=t, tk=t),
        grid=(b, heads, n, n),
        in_specs=in_specs,
        out_specs=pl.BlockSpec((1, t, dv), lambda i, h, qi, ki: (i, qi, h)),
        out_shape=jax.ShapeDtypeStruct((b, s, heads * dv), BF16),
        scratch_shapes=scratch,
        compiler_params=_params("parallel", "parallel", "parallel", "arbitrary"),
        name="fox_attention" if bias is not None else "mla_attention",
    )(*args)


def _mla_rope(x, mc, ma, mb):
    return x * mc + pltpu.roll(x, LANES - MLA_ROPE // 2, 1) * ma + pltpu.roll(x, MLA_ROPE // 2, 1) * mb


def _mla_q_kernel(c_ref, g_ref, w_ref, mc_ref, ma_ref, mb_ref, o_ref):
    x = c_ref[...]
    xn = (x * lax.rsqrt(jnp.mean(x * x, axis=-1, keepdims=True) + EPS) * g_ref[...]).astype(BF16)
    q = _dot(xn, w_ref[...])
    scale = (MLA_NOPE + MLA_ROPE) ** -0.5
    mc, ma, mb = mc_ref[...], ma_ref[...], mb_ref[...]
    for h in range(MLA_HEADS):
        lo = h * MLA_QK_PAD
        o_ref[:, lo:lo + MLA_NOPE] = (q[:, lo:lo + MLA_NOPE] * scale).astype(o_ref.dtype)
        pe = _mla_rope(q[:, lo + MLA_NOPE:lo + MLA_QK_PAD], mc, ma, mb)
        o_ref[:, lo + MLA_NOPE:lo + MLA_QK_PAD] = (pe * scale).astype(o_ref.dtype)


def mla_queries(za2, q_norm, w_uq, mc, ma, mb, *, bm=512):
    t = za2.shape[0]
    bm = _tile(t, bm, SUBLANES)
    w = w_uq.reshape(MLA_Q_LORA, MLA_HEADS, MLA_NOPE + MLA_ROPE)
    w = jnp.pad(w, ((0, 0), (0, 0), (0, MLA_QK_PAD - MLA_NOPE - MLA_ROPE)))
    w = w.reshape(MLA_Q_LORA, MLA_HEADS * MLA_QK_PAD).astype(BF16)
    tab = pl.BlockSpec((bm, LANES), lambda i: (i, 0))
    return pl.pallas_call(
        _mla_q_kernel,
        grid=(t // bm,),
        in_specs=[pl.BlockSpec((bm, MLA_Q_LORA), lambda i: (i, A_CQ // MLA_Q_LORA)),
                  pl.BlockSpec((1, MLA_Q_LORA), lambda i: (0, 0)),
                  pl.BlockSpec(w.shape, lambda i: (0, 0)),
                  tab, tab, tab],
        out_specs=pl.BlockSpec((bm, MLA_HEADS * MLA_QK_PAD), lambda i: (i, 0)),
        out_shape=jax.ShapeDtypeStruct((t, MLA_HEADS * MLA_QK_PAD), BF16),
        compiler_params=_params("parallel"),
        name="mla_queries",
    )(za2, q_norm.reshape(1, -1), w, mc, ma, mb)


def _mla_kv_kernel(c_ref, kr_ref, g_ref, wk_ref, wv_ref, mc_ref, ma_ref, mb_ref, k_ref, v_ref):
    x = c_ref[...]
    xn = (x * lax.rsqrt(jnp.mean(x * x, axis=-1, keepdims=True) + EPS) * g_ref[...]).astype(BF16)
    k_nope = _dot(xn, wk_ref[...])
    v_ref[...] = _dot(xn, wv_ref[...]).astype(v_ref.dtype)
    k_pe = _mla_rope(kr_ref[...], mc_ref[...], ma_ref[...], mb_ref[...]).astype(k_ref.dtype)
    for h in range(MLA_HEADS):
        lo = h * MLA_QK_PAD
        k_ref[:, lo:lo + MLA_NOPE] = k_nope[:, h * MLA_NOPE:(h + 1) * MLA_NOPE].astype(k_ref.dtype)
        k_ref[:, lo + MLA_NOPE:lo + MLA_QK_PAD] = k_pe


def mla_keys_values(za2, kv_norm, w_ukv, mc, ma, mb, *, bm=512):
    t = za2.shape[0]
    bm = _tile(t, bm, SUBLANES)
    w = w_ukv.reshape(MLA_KV_LORA, MLA_HEADS, MLA_NOPE + MLA_DV)
    wk = w[:, :, :MLA_NOPE].reshape(MLA_KV_LORA, MLA_HEADS * MLA_NOPE).astype(BF16)
    wv = w[:, :, MLA_NOPE:].reshape(MLA_KV_LORA, MLA_HEADS * MLA_DV).astype(BF16)
    tab = pl.BlockSpec((bm, LANES), lambda i: (i, 0))
    return pl.pallas_call(
        _mla_kv_kernel,
        grid=(t // bm,),
        in_specs=[pl.BlockSpec((bm, MLA_KV_LORA), lambda i: (i, A_CKV // MLA_KV_LORA)),
                  pl.BlockSpec((bm, LANES), lambda i: (i, A_KR // LANES)),
                  pl.BlockSpec((1, MLA_KV_LORA), lambda i: (0, 0)),
                  pl.BlockSpec(wk.shape, lambda i: (0, 0)),
                  pl.BlockSpec(wv.shape, lambda i: (0, 0)),
                  tab, tab, tab],
        out_specs=[pl.BlockSpec((bm, MLA_HEADS * MLA_QK_PAD), lambda i: (i, 0)),
                   pl.BlockSpec((bm, MLA_HEADS * MLA_DV), lambda i: (i, 0))],
        out_shape=[jax.ShapeDtypeStruct((t, MLA_HEADS * MLA_QK_PAD), BF16),
                   jax.ShapeDtypeStruct((t, MLA_HEADS * MLA_DV), BF16)],
        compiler_params=_params("parallel"),
        name="mla_keys_values",
    )(za2, za2, kv_norm.reshape(1, -1), wk, wv, mc, ma, mb)


def _split_w_in(w_in):
    d = w_in.shape[0]
    rq_rk_end = 2 * RET_HEADS * RET_DK
    bf_end = rq_rk_end + 2 * RET_HEADS * RET_DV + 3 * FOX_HEADS * FOX_DH
    ff_end = bf_end + FOX_HEADS
    used = rq_rk_end + (w_in.shape[1] - ff_end) + FOX_HEADS
    w_a = jnp.concatenate([w_in[:, :rq_rk_end], w_in[:, ff_end:], w_in[:, bf_end:ff_end],
                           jnp.zeros((d, A_WIDTH - used), w_in.dtype)], axis=1)
    w_b = w_in[:, rq_rk_end:bf_end]
    return w_a.astype(BF16), w_b.astype(BF16)


def _ffn(h, norm, w13, w2):
    u = rmsnorm(h, norm, BF16)
    g = swiglu_up(u, w13.astype(BF16))
    return matmul(g, w2.astype(BF16), res=h, scale=0.5, bk=2048, name="ffn_down")


def _mixer(h, batch, tables, mix_norm, w_in, b_forget, ret_norm, mla_q_norm, mla_kv_norm, w_uq, w_ukv,
           w_up_ret, w_up_fox, w_up_mla, w_gate, b_gate, w_out):
    rc, rs, mc, ma, mb = tables
    t, d = h.shape
    s = t // batch
    u = rmsnorm(h, mix_norm, BF16)
    w_a, w_b = _split_w_in(w_in)
    za = matmul(u, w_a, out_dtype=F32, name="in_proj_f32")
    zb = matmul(u, w_b, out_dtype=BF16, name="in_proj_bf16")
    za3 = za.reshape(batch, s, A_WIDTH)
    zb3 = zb.reshape(batch, s, B_WIDTH)
    tab3 = lambda x: x.reshape(batch, s, LANES)

    ro = retention(za3, zb3, tab3(rc), tab3(rs), ret_norm)
    fc, fct = forget_cumsum(za3, b_forget)
    fo = attention(zb3, B_FQ // FOX_DH, zb3, B_FK // FOX_DH, zb3, B_FV // FOX_DH,
                   heads=FOX_HEADS, dk=FOX_DH, dv=FOX_DH, bias=(fc, fct), chunked_mask=False,
                   scale=FOX_DH ** -0.5)
    qm = mla_queries(za, mla_q_norm, w_uq, mc, ma, mb)
    km, vm = mla_keys_values(za, mla_kv_norm, w_ukv, mc, ma, mb)
    mo = attention(qm.reshape(batch, s, -1), 0, km.reshape(batch, s, -1), 0, vm.reshape(batch, s, -1), 0,
                   heads=MLA_HEADS, dk=MLA_QK_PAD, dv=MLA_DV, chunked_mask=True)

    merged = gated_merge(u, ro.reshape(t, -1), fo.reshape(t, -1), mo.reshape(t, -1),
                         w_gate.astype(BF16), b_gate, w_up_ret.astype(BF16), w_up_fox.astype(BF16),
                         w_up_mla.astype(BF16))
    return matmul(merged, w_out.astype(BF16), res=h, scale=1.0, name="out_proj")


def kernel(x, positions, ffn1_norm, ffn1_w13, ffn1_w2, mix_norm, w_in, b_forget, ret_norm, mla_q_norm,
           mla_kv_norm, w_uq, w_ukv, w_up_ret, w_up_fox, w_up_mla, w_gate, b_gate, w_out, ffn2_norm,
           ffn2_w13, ffn2_w2, final_norm):
    batch, s, d = x.shape
    h = x.reshape(batch * s, d)
    tables = rope_tables(positions)
    for l in range(ffn1_norm.shape[0]):
        h = _ffn(h, ffn1_norm[l], ffn1_w13[l], ffn1_w2[l])
        h = _mixer(h, batch, tables, mix_norm[l], w_in[l], b_forget[l], ret_norm[l], mla_q_norm[l],
                   mla_kv_norm[l], w_uq[l], w_ukv[l], w_up_ret[l], w_up_fox[l], w_up_mla[l], w_gate[l],
                   b_gate[l], w_out[l])
        h = _ffn(h, ffn2_norm[l], ffn2_w13[l], ffn2_w2[l])
    return rmsnorm(h, final_norm, x.dtype).reshape(batch, s, d)
```

```python
import functools
import math

import jax
import jax.numpy as jnp
from jax import lax
from jax.experimental import pallas as pl
from jax.experimental.pallas import tpu as pltpu

F32 = jnp.float32
BF16 = jnp.bfloat16

CHUNK = 64
RET_HEADS, RET_DK, RET_DV = 8, 128, 256
FOX_HEADS, FOX_DH = 8, 128
MLA_HEADS, MLA_Q_LORA, MLA_KV_LORA = 8, 1024, 512
MLA_NOPE, MLA_ROPE, MLA_DV = 128, 64, 128
MLA_QK_PAD = 256
ROPE_BASE = 10000.0
EPS = 1e-6
NEG_BIG = -1e30
LOG2_E = math.log2(math.e)
N_SPLIT = 3
ATTN_HEADS_PER_STEP = 8

LANES = 128
SUBLANES = 8
VMEM_LIMIT_BYTES = 56 * 1024 * 1024

A_RQ, A_RK, A_CQ, A_CKV, A_KR = 0, 1024, 2048, 3072, 3584
A_FF = A_KR + MLA_ROPE
A_WIDTH = 4096
FF_LANE = A_FF - A_KR
B_RV, B_RG, B_FQ, B_FK, B_FV = 0, 2048, 4096, 5120, 6144
B_WIDTH = 7168


def _tile(dim, pref, quantum=LANES):
    if dim <= pref:
        return dim
    t = (pref // quantum) * quantum
    while t > quantum and dim % t:
        t -= quantum
    assert dim % t == 0, (dim, pref)
    return t


def _params(*sem):
    return pltpu.CompilerParams(dimension_semantics=sem, vmem_limit_bytes=VMEM_LIMIT_BYTES)


def _dot(a, b):
    return jnp.dot(a, b, preferred_element_type=F32)


def _dot_nt(a, b):
    return lax.dot_general(a, b, (((1,), (1,)), ((), ())), preferred_element_type=F32)


def _dot_tn(a, b):
    return lax.dot_general(a, b, (((0,), (0,)), ((), ())), preferred_element_type=F32)


def _rmsnorm_kernel(x_ref, g_ref, o_ref):
    x = x_ref[...]
    ms = jnp.mean(x * x, axis=-1, keepdims=True)
    o_ref[...] = (x * lax.rsqrt(ms + EPS) * g_ref[...]).astype(o_ref.dtype)


def rmsnorm(x, g, out_dtype):
    m, d = x.shape
    bm = _tile(m, 512, SUBLANES)
    return pl.pallas_call(
        _rmsnorm_kernel,
        grid=(m // bm,),
        in_specs=[pl.BlockSpec((bm, d), lambda i: (i, 0)),
                  pl.BlockSpec((1, d), lambda i: (0, 0))],
        out_specs=pl.BlockSpec((bm, d), lambda i: (i, 0)),
        out_shape=jax.ShapeDtypeStruct((m, d), out_dtype),
        compiler_params=_params("parallel"),
        name="rmsnorm",
    )(x, g.reshape(1, d))


def _mm_kernel(a_ref, w_ref, *refs, nk, scale, has_res):
    if has_res:
        r_ref, o_ref, *scratch = refs
    else:
        o_ref, *scratch = refs

    def finish(acc):
        if has_res:
            o_ref[...] = r_ref[...] + scale * acc
        else:
            o_ref[...] = acc.astype(o_ref.dtype)

    part = _dot(a_ref[...], w_ref[...])
    if nk == 1:
        finish(part)
        return
    acc_ref, = scratch
    k = pl.program_id(2)

    @pl.when(k == 0)
    def _():
        acc_ref[...] = part

    @pl.when(jnp.logical_and(k > 0, k < nk - 1))
    def _():
        acc_ref[...] += part

    @pl.when(k == nk - 1)
    def _():
        finish(acc_ref[...] + part)


def matmul(a, w, layer, *, col=0, n=None, out_dtype=F32, res=None, scale=1.0, bm=1024, bn=1024, bk=None,
           name="matmul"):
    m, kd = a.shape
    n = w.shape[2] if n is None else n
    bm, bn = _tile(m, bm, SUBLANES), _tile(math.gcd(n, col) if col else n, bn)
    bk = kd if bk is None else _tile(kd, bk)
    nk = kd // bk
    first = col // bn
    in_specs = [pl.BlockSpec((bm, bk), lambda i, j, k: (i, k)),
                pl.BlockSpec((None, bk, bn), lambda i, j, k: (layer, k, first + j))]
    args = [a, w]
    if res is not None:
        in_specs.append(pl.BlockSpec((bm, bn), lambda i, j, k: (i, j)))
        args.append(res)
    return pl.pallas_call(
        functools.partial(_mm_kernel, nk=nk, scale=scale, has_res=res is not None),
        grid=(m // bm, n // bn, nk),
        in_specs=in_specs,
        out_specs=pl.BlockSpec((bm, bn), lambda i, j, k: (i, j)),
        out_shape=jax.ShapeDtypeStruct((m, n), out_dtype),
        scratch_shapes=[pltpu.VMEM((bm, bn), F32)] if nk > 1 else [],
        compiler_params=_params("parallel", "parallel", "arbitrary"),
        name=name,
    )(*args)


def _swiglu_kernel(u_ref, w1_ref, w3_ref, o_ref):
    u = u_ref[...]
    a = _dot(u, w1_ref[...])
    b = _dot(u, w3_ref[...])
    o_ref[...] = (a * jax.nn.sigmoid(a) * b).astype(o_ref.dtype)


def swiglu_up(u, w13, layer, *, bm=1024, bn=512):
    m, d = u.shape
    f = w13.shape[2] // 2
    bm, bn = _tile(m, bm, SUBLANES), _tile(f, bn)
    nf = f // bn
    return pl.pallas_call(
        _swiglu_kernel,
        grid=(m // bm, nf),
        in_specs=[pl.BlockSpec((bm, d), lambda i, j: (i, 0)),
                  pl.BlockSpec((None, d, bn), lambda i, j: (layer, 0, j)),
                  pl.BlockSpec((None, d, bn), lambda i, j: (layer, 0, j + nf))],
        out_specs=pl.BlockSpec((bm, bn), lambda i, j: (i, j)),
        out_shape=jax.ShapeDtypeStruct((m, f), BF16),
        compiler_params=_params("parallel", "parallel"),
        name="swiglu_up",
    )(u, w13, w13)


def _merge_kernel(u_ref, ro_ref, fo_ref, mo_ref, wg_ref, bg_ref, wr_ref, wf_ref, wm_ref, o_ref):
    u = u_ref[...]
    total = None
    for i, (x_ref, w_ref) in enumerate(((ro_ref, wr_ref), (fo_ref, wf_ref), (mo_ref, wm_ref))):
        gate = jax.nn.sigmoid(_dot(u, wg_ref[i]) + bg_ref[i:i + 1, :])
        term = gate * _dot(x_ref[...], w_ref[...])
        total = term if total is None else total + term
    o_ref[...] = total.astype(o_ref.dtype)


def gated_merge(u, ro, fo, mo, w_gate, b_gate, w_up_ret, w_up_fox, w_up_mla, layer, *, bm=512, bn=256):
    m, d = u.shape
    n = w_gate.shape[-1]
    bm, bn = _tile(m, bm, SUBLANES), _tile(n, bn)
    row = lambda width: pl.BlockSpec((bm, width), lambda i, j: (i, 0))
    col = lambda depth: pl.BlockSpec((None, depth, bn), lambda i, j: (layer, 0, j))
    return pl.pallas_call(
        _merge_kernel,
        grid=(m // bm, n // bn),
        in_specs=[row(d), row(ro.shape[1]), row(fo.shape[1]), row(mo.shape[1]),
                  pl.BlockSpec((None, 3, d, bn), lambda i, j: (layer, 0, 0, j)),
                  pl.BlockSpec((None, 3, bn), lambda i, j: (layer, 0, j)),
                  col(ro.shape[1]), col(fo.shape[1]), col(mo.shape[1])],
        out_specs=pl.BlockSpec((bm, bn), lambda i, j: (i, j)),
        out_shape=jax.ShapeDtypeStruct((m, n), BF16),
        compiler_params=_params("parallel", "parallel"),
        name="gated_merge",
    )(u, ro, fo, mo, w_gate, b_gate, w_up_ret, w_up_fox, w_up_mla)


def _rope_table_kernel(pos_ref, inv_ref, coef_ref, rc_ref, rs_ref, mc_ref, ma_ref, mb_ref):
    p = pos_ref[...].astype(F32)
    ang_r = p * inv_ref[0:1, :]
    ang_m = p * inv_ref[1:2, :]
    rc_ref[...] = jnp.cos(ang_r)
    rs_ref[...] = jnp.sin(ang_r) * coef_ref[0:1, :]
    sin_m = jnp.sin(ang_m)
    mc_ref[...] = jnp.cos(ang_m) * coef_ref[1:2, :]
    ma_ref[...] = sin_m * coef_ref[2:3, :]
    mb_ref[...] = sin_m * coef_ref[3:4, :]


def rope_tables(positions):
    t = positions.size
    h_r, h_m = RET_DK // 2, MLA_ROPE // 2
    inv_r = ROPE_BASE ** (-jnp.arange(h_r, dtype=F32) / h_r)
    inv_m = ROPE_BASE ** (-jnp.arange(h_m, dtype=F32) / h_m)
    zeros_m = jnp.zeros((LANES - 2 * h_m,), F32)
    ones_m = jnp.ones((h_m,), F32)
    inv = jnp.stack([jnp.concatenate([inv_r, inv_r]),
                     jnp.concatenate([inv_m, inv_m, zeros_m])])
    coef = jnp.stack([
        jnp.concatenate([-jnp.ones((h_r,), F32), jnp.ones((h_r,), F32)]),
        jnp.concatenate([ones_m, ones_m, zeros_m]),
        jnp.concatenate([-ones_m, 0 * ones_m, zeros_m]),
        jnp.concatenate([0 * ones_m, ones_m, zeros_m]),
    ])
    bm = _tile(t, 1024, SUBLANES)
    tab = jax.ShapeDtypeStruct((t, LANES), F32)
    return pl.pallas_call(
        _rope_table_kernel,
        grid=(t // bm,),
        in_specs=[pl.BlockSpec((bm, 1), lambda i: (i, 0)),
                  pl.BlockSpec((2, LANES), lambda i: (0, 0)),
                  pl.BlockSpec((4, LANES), lambda i: (0, 0))],
        out_specs=[pl.BlockSpec((bm, LANES), lambda i: (i, 0))] * 5,
        out_shape=[tab] * 5,
        compiler_params=_params("parallel"),
        name="rope_tables",
    )(positions.reshape(t, 1), inv, coef)


def _ret_log_gamma(h):
    return math.log1p(-2.0 ** (-5.0 - h))


def _retention_kernel(q_ref, k_ref, v_ref, g_ref, rc_ref, rs_ref, norm_ref, o_ref, state_ref, decay_ref,
                      *, blk):
    first = jnp.logical_and(pl.program_id(0) == 0, pl.program_id(1) == 0)

    @pl.when(first)
    def _():
        row = lax.broadcasted_iota(jnp.int32, (blk, blk), 0)
        col = lax.broadcasted_iota(jnp.int32, (blk, blk), 1)
        dist = jnp.abs(row - col).astype(F32)
        visible = (col // CHUNK) <= (row // CHUNK)
        for h in range(RET_HEADS):
            decay_ref[h] = jnp.where(visible, jnp.exp(_ret_log_gamma(h) * dist), 0.0)

    @pl.when(pl.program_id(1) == 0)
    def _():
        state_ref[...] = jnp.zeros_like(state_ref)

    rc = rc_ref[0]
    rs = rs_ref[0]
    idx = lax.broadcasted_iota(jnp.int32, (blk, 1), 0).astype(F32)
    for h in range(RET_HEADS):
        lg = _ret_log_gamma(h)
        qk = slice(h * RET_DK, (h + 1) * RET_DK)
        vv = slice(h * RET_DV, (h + 1) * RET_DV)
        q = q_ref[0, :, qk]
        k = k_ref[0, :, qk]
        q = (q * rc + pltpu.roll(q, RET_DK // 2, 1) * rs) * (RET_DK ** -0.5)
        k = k * rc + pltpu.roll(k, RET_DK // 2, 1) * rs
        v = v_ref[0, :, vv]
        state = state_ref[h]
        scores = _dot_nt(q.astype(BF16), k.astype(BF16)) * decay_ref[h]
        q_in = (q * jnp.exp(lg * (idx + 1.0))).astype(BF16)
        out = _dot(scores.astype(BF16), v) + _dot(q_in, state.astype(BF16))
        k_out = (k * jnp.exp(lg * (blk - 1.0 - idx))).astype(BF16)
        state_ref[h] = state * math.exp(lg * blk) + _dot_tn(k_out, v)
        out = out * lax.rsqrt(jnp.mean(out * out, axis=-1, keepdims=True) + EPS)
        gate = g_ref[0, :, vv].astype(F32)
        out = out * norm_ref[:, vv] * (gate * jax.nn.sigmoid(gate))
        o_ref[0, :, vv] = out.astype(o_ref.dtype)


def retention(za, zb, rc, rs, ret_norm, *, blk=256):
    b, s, _ = za.shape
    blk = _tile(s, blk, CHUNK)
    qw, vw = RET_HEADS * RET_DK, RET_HEADS * RET_DV
    return pl.pallas_call(
        functools.partial(_retention_kernel, blk=blk),
        grid=(b, s // blk),
        in_specs=[pl.BlockSpec((1, blk, qw), lambda i, j: (i, j, A_RQ // qw)),
                  pl.BlockSpec((1, blk, qw), lambda i, j: (i, j, A_RK // qw)),
                  pl.BlockSpec((1, blk, vw), lambda i, j: (i, j, B_RV // vw)),
                  pl.BlockSpec((1, blk, vw), lambda i, j: (i, j, B_RG // vw)),
                  pl.BlockSpec((1, blk, LANES), lambda i, j: (i, j, 0)),
                  pl.BlockSpec((1, blk, LANES), lambda i, j: (i, j, 0)),
                  pl.BlockSpec((1, vw), lambda i, j: (0, 0))],
        out_specs=pl.BlockSpec((1, blk, vw), lambda i, j: (i, j, 0)),
        out_shape=jax.ShapeDtypeStruct((b, s, vw), BF16),
        scratch_shapes=[pltpu.VMEM((RET_HEADS, RET_DK, RET_DV), F32),
                        pltpu.VMEM((RET_HEADS, blk, blk), F32)],
        compiler_params=_params("arbitrary", "arbitrary"),
        name="retention",
    )(za, za, zb, zb, rc, rs, ret_norm.reshape(1, vw))


def _split3(x):
    hi = x.astype(BF16).astype(F32)
    rest = x - hi
    mid = rest.astype(BF16).astype(F32)
    lo = (rest - mid).astype(BF16).astype(F32)
    return hi, mid, lo


def _forget_cumsum_kernel(z_ref, bias_ref, qa_ref, ka_ref, carry_ref, *, rows):
    @pl.when(pl.program_id(1) == 0)
    def _():
        carry_ref[...] = jnp.zeros_like(carry_ref)

    lane = lax.broadcasted_iota(jnp.int32, (rows, LANES), 1)
    is_gate = jnp.logical_and(lane >= FF_LANE, lane < FF_LANE + FOX_HEADS)
    x = z_ref[0] + bias_ref[...]
    log_f = jnp.minimum(x, 0.0) - jnp.log(1.0 + jnp.exp(-jnp.abs(x)))
    log_f = jnp.where(is_gate, log_f * LOG2_E, 0.0)
    tri = (lax.broadcasted_iota(jnp.int32, (rows, rows), 0)
           >= lax.broadcasted_iota(jnp.int32, (rows, rows), 1)).astype(BF16)
    cum = carry_ref[...]
    for piece in _split3(log_f):
        cum = cum + _dot(tri, piece.astype(BF16))
    carry_ref[...] = cum[rows - 1:rows, :]
    one = jnp.where(lane < 2 * N_SPLIT, 1.0, 0.0)
    for h in range(FOX_HEADS):
        col = jnp.sum(jnp.where(lane == FF_LANE + h, cum, 0.0), axis=1, keepdims=True)
        pieces = _split3(jnp.broadcast_to(col, (rows, LANES)))
        qa = jnp.where(lane < N_SPLIT, 0.0, one)
        ka = jnp.where(lane < N_SPLIT, one, 0.0)
        for i, piece in enumerate(pieces):
            qa = jnp.where(lane == i, piece, qa)
            ka = jnp.where(lane == N_SPLIT + i, -piece, ka)
        qa_ref[0, :, h * LANES:(h + 1) * LANES] = qa.astype(qa_ref.dtype)
        ka_ref[0, :, h * LANES:(h + 1) * LANES] = ka.astype(ka_ref.dtype)


def forget_bias_lanes(za, b_forget, *, rows=256):
    b, s, _ = za.shape
    rows = _tile(s, rows)
    bias = jnp.zeros((1, LANES), F32).at[0, FF_LANE:FF_LANE + FOX_HEADS].set(b_forget)
    aux = jax.ShapeDtypeStruct((b, s, FOX_HEADS * LANES), BF16)
    return pl.pallas_call(
        functools.partial(_forget_cumsum_kernel, rows=rows),
        grid=(b, s // rows),
        in_specs=[pl.BlockSpec((1, rows, LANES), lambda i, j: (i, j, A_KR // LANES)),
                  pl.BlockSpec((1, LANES), lambda i, j: (0, 0))],
        out_specs=[pl.BlockSpec((1, rows, FOX_HEADS * LANES), lambda i, j: (i, j, 0))] * 2,
        out_shape=[aux, aux],
        scratch_shapes=[pltpu.VMEM((1, LANES), F32)],
        compiler_params=_params("arbitrary", "arbitrary"),
        name="forget_bias_lanes",
    )(za, bias)


def _attn_kernel(qi_ref, ki_ref, *refs, has_aux, chunked_mask, hp, dk, dv, t):
    if has_aux:
        q_ref, qa_ref, k_ref, ka_ref, v_ref, o_ref, m_ref, l_ref, acc_ref = refs
    else:
        q_ref, k_ref, v_ref, o_ref, m_ref, l_ref, acc_ref = refs
    pair = pl.program_id(2)
    qi, ki = qi_ref[pair], ki_ref[pair]

    @pl.when(ki == 0)
    def _():
        m_ref[...] = jnp.full_like(m_ref, NEG_BIG)
        l_ref[...] = jnp.zeros_like(l_ref)
        acc_ref[...] = jnp.zeros_like(acc_ref)

    def step(diagonal):
        if diagonal:
            row = lax.broadcasted_iota(jnp.int32, (t, t), 0)
            col = lax.broadcasted_iota(jnp.int32, (t, t), 1)
            keep = (col // CHUNK) <= (row // CHUNK) if chunked_mask else col <= row
        for h in range(hp):
            q = q_ref[0, :, h * dk:(h + 1) * dk]
            k = k_ref[0, :, h * dk:(h + 1) * dk]
            if has_aux:
                q = jnp.concatenate([q, qa_ref[0, :, h * LANES:(h + 1) * LANES]], axis=1)
                k = jnp.concatenate([k, ka_ref[0, :, h * LANES:(h + 1) * LANES]], axis=1)
            s = _dot_nt(q, k)
            if diagonal:
                s = jnp.where(keep, s, NEG_BIG)
            m_old = m_ref[h]
            m_new = jnp.maximum(m_old, jnp.max(s, axis=1, keepdims=True))
            alpha = jnp.exp2(m_old - m_new)
            p = jnp.exp2(s - pltpu.repeat(m_new, t // LANES, 1))
            l_ref[h] = alpha * l_ref[h] + jnp.sum(p, axis=1, keepdims=True)
            acc_ref[h] = alpha * acc_ref[h] + _dot(p.astype(BF16), v_ref[0, :, h * dv:(h + 1) * dv])
            m_ref[h] = m_new

    @pl.when(ki < qi)
    def _():
        step(False)

    @pl.when(ki == qi)
    def _():
        step(True)
        for h in range(hp):
            o_ref[0, :, h * dv:(h + 1) * dv] = (acc_ref[h] / l_ref[h]).astype(o_ref.dtype)


def attention(q_arr, q_col, k_arr, k_col, v_arr, v_col, *, heads, dk, dv, aux=None, chunked_mask,
              tile=512, hp=ATTN_HEADS_PER_STEP):
    assert dv == LANES and heads % hp == 0
    b, s, _ = q_arr.shape
    t = _tile(s, tile)
    n = s // t
    pairs = [(i, j) for i in range(n) for j in range(i + 1)]
    qi_tab = jnp.asarray([p[0] for p in pairs], jnp.int32)
    ki_tab = jnp.asarray([p[1] for p in pairs], jnp.int32)

    def spec(width, col, table):
        assert col % (hp * width) == 0
        first = col // (hp * width)
        if table == "q":
            return pl.BlockSpec((1, t, hp * width), lambda i, g, p, qt, kt: (i, qt[p], first + g))
        return pl.BlockSpec((1, t, hp * width), lambda i, g, p, qt, kt: (i, kt[p], first + g))

    if aux is None:
        in_specs = [spec(dk, q_col, "q"), spec(dk, k_col, "k"), spec(dv, v_col, "k")]
        args = [q_arr, k_arr, v_arr]
    else:
        in_specs = [spec(dk, q_col, "q"), spec(LANES, 0, "q"), spec(dk, k_col, "k"), spec(LANES, 0, "k"),
                    spec(dv, v_col, "k")]
        args = [q_arr, aux[0], k_arr, aux[1], v_arr]
    stat = pltpu.VMEM((hp, t, LANES), F32)
    return pl.pallas_call(
        functools.partial(_attn_kernel, has_aux=aux is not None, chunked_mask=chunked_mask, hp=hp, dk=dk,
                          dv=dv, t=t),
        grid_spec=pltpu.PrefetchScalarGridSpec(
            num_scalar_prefetch=2,
            grid=(b, heads // hp, len(pairs)),
            in_specs=in_specs,
            out_specs=spec(dv, 0, "q"),
            scratch_shapes=[stat, stat, pltpu.VMEM((hp, t, dv), F32)]),
        out_shape=jax.ShapeDtypeStruct((b, s, heads * dv), BF16),
        compiler_params=_params("parallel", "parallel", "arbitrary"),
        name="fox_attention" if aux is not None else "mla_attention",
    )(qi_tab, ki_tab, *args)


def _mla_rope(x, mc, ma, mb):
    return x * mc + pltpu.roll(x, LANES - MLA_ROPE // 2, 1) * ma + pltpu.roll(x, MLA_ROPE // 2, 1) * mb


def _mla_q_kernel(c_ref, g_ref, w_ref, mc_ref, ma_ref, mb_ref, o_ref):
    x = c_ref[...]
    xn = (x * lax.rsqrt(jnp.mean(x * x, axis=-1, keepdims=True) + EPS) * g_ref[...]).astype(BF16)
    q = _dot(xn, w_ref[...])
    scale = (MLA_NOPE + MLA_ROPE) ** -0.5 * LOG2_E
    mc, ma, mb = mc_ref[...], ma_ref[...], mb_ref[...]
    for h in range(MLA_HEADS):
        lo = h * MLA_QK_PAD
        o_ref[:, lo:lo + MLA_NOPE] = (q[:, lo:lo + MLA_NOPE] * scale).astype(o_ref.dtype)
        pe = _mla_rope(q[:, lo + MLA_NOPE:lo + MLA_QK_PAD], mc, ma, mb)
        o_ref[:, lo + MLA_NOPE:lo + MLA_QK_PAD] = (pe * scale).astype(o_ref.dtype)


def mla_queries(za2, q_norm, w_uq_pad, layer, mc, ma, mb, *, bm=512):
    t = za2.shape[0]
    bm = _tile(t, bm, SUBLANES)
    tab = pl.BlockSpec((bm, LANES), lambda i: (i, 0))
    return pl.pallas_call(
        _mla_q_kernel,
        grid=(t // bm,),
        in_specs=[pl.BlockSpec((bm, MLA_Q_LORA), lambda i: (i, A_CQ // MLA_Q_LORA)),
                  pl.BlockSpec((1, MLA_Q_LORA), lambda i: (0, 0)),
                  pl.BlockSpec((None,) + w_uq_pad.shape[1:], lambda i: (layer, 0, 0)),
                  tab, tab, tab],
        out_specs=pl.BlockSpec((bm, MLA_HEADS * MLA_QK_PAD), lambda i: (i, 0)),
        out_shape=jax.ShapeDtypeStruct((t, MLA_HEADS * MLA_QK_PAD), BF16),
        compiler_params=_params("parallel"),
        name="mla_queries",
    )(za2, q_norm.reshape(1, -1), w_uq_pad, mc, ma, mb)


def _mla_kv_kernel(c_ref, kr_ref, g_ref, wk_ref, wv_ref, mc_ref, ma_ref, mb_ref, k_ref, v_ref):
    x = c_ref[...]
    xn = (x * lax.rsqrt(jnp.mean(x * x, axis=-1, keepdims=True) + EPS) * g_ref[...]).astype(BF16)
    k_nope = _dot(xn, wk_ref[...])
    v_ref[...] = _dot(xn, wv_ref[...]).astype(v_ref.dtype)
    k_pe = _mla_rope(kr_ref[...], mc_ref[...], ma_ref[...], mb_ref[...]).astype(k_ref.dtype)
    for h in range(MLA_HEADS):
        lo = h * MLA_QK_PAD
        k_ref[:, lo:lo + MLA_NOPE] = k_nope[:, h * MLA_NOPE:(h + 1) * MLA_NOPE].astype(k_ref.dtype)
        k_ref[:, lo + MLA_NOPE:lo + MLA_QK_PAD] = k_pe


def mla_keys_values(za2, kv_norm, wk, wv, layer, mc, ma, mb, *, bm=512):
    t = za2.shape[0]
    bm = _tile(t, bm, SUBLANES)
    tab = pl.BlockSpec((bm, LANES), lambda i: (i, 0))
    w_spec = pl.BlockSpec((None,) + wk.shape[1:], lambda i: (layer, 0, 0))
    return pl.pallas_call(
        _mla_kv_kernel,
        grid=(t // bm,),
        in_specs=[pl.BlockSpec((bm, MLA_KV_LORA), lambda i: (i, A_CKV // MLA_KV_LORA)),
                  pl.BlockSpec((bm, LANES), lambda i: (i, A_KR // LANES)),
                  pl.BlockSpec((1, MLA_KV_LORA), lambda i: (0, 0)),
                  w_spec, w_spec, tab, tab, tab],
        out_specs=[pl.BlockSpec((bm, MLA_HEADS * MLA_QK_PAD), lambda i: (i, 0)),
                   pl.BlockSpec((bm, MLA_HEADS * MLA_DV), lambda i: (i, 0))],
        out_shape=[jax.ShapeDtypeStruct((t, MLA_HEADS * MLA_QK_PAD), BF16),
                   jax.ShapeDtypeStruct((t, MLA_HEADS * MLA_DV), BF16)],
        compiler_params=_params("parallel"),
        name="mla_keys_values",
    )(za2, za2, kv_norm.reshape(1, -1), wk, wv, mc, ma, mb)


def _prepare_weights(p):
    w_in = p["w_in"]
    depth, d, width = w_in.shape
    rq_rk_end = 2 * RET_HEADS * RET_DK
    bf_end = rq_rk_end + B_WIDTH
    ff_end = bf_end + FOX_HEADS
    fq0 = rq_rk_end + B_FQ
    col_scale = jnp.ones((width,), F32).at[fq0:fq0 + FOX_HEADS * FOX_DH].set(FOX_DH ** -0.5 * LOG2_E)
    w_in_b =(w_in * col_scale).astype(BF16)
    used = rq_rk_end + (width - ff_end) + FOX_HEADS
    w_in_a = jnp.concatenate([w_in_b[:, :, :rq_rk_end], w_in_b[:, :, ff_end:], w_in_b[:, :, bf_end:ff_end],
                              jnp.zeros((depth, d, A_WIDTH - used), BF16)], axis=2)
    w_uq = p["w_uq"].reshape(depth, MLA_Q_LORA, MLA_HEADS, MLA_NOPE + MLA_ROPE)
    w_uq = jnp.pad(w_uq, ((0, 0), (0, 0), (0, 0), (0, MLA_QK_PAD - MLA_NOPE - MLA_ROPE)))
    w_ukv = p["w_ukv"].reshape(depth, MLA_KV_LORA, MLA_HEADS, MLA_NOPE + MLA_DV)
    out = {k: p[k].astype(BF16) for k in ("ffn1_w13", "ffn1_w2", "ffn2_w13", "ffn2_w2", "w_gate", "w_out",
                                          "w_up_ret", "w_up_fox", "w_up_mla")}
    out.update(
        w_in_a=w_in_a, w_in_b=w_in_b, w_in_b_col=rq_rk_end,
        w_uq=w_uq.reshape(depth, MLA_Q_LORA, MLA_HEADS * MLA_QK_PAD).astype(BF16),
        w_uk=w_ukv[..., :MLA_NOPE].reshape(depth, MLA_KV_LORA, MLA_HEADS * MLA_NOPE).astype(BF16),
        w_uv=w_ukv[..., MLA_NOPE:].reshape(depth, MLA_KV_LORA, MLA_HEADS * MLA_DV).astype(BF16))
    return out


def _ffn(h, norm, w13, w2, layer):
    u = rmsnorm(h, norm, BF16)
    g = swiglu_up(u, w13, layer)
    return matmul(g, w2, layer, res=h, scale=0.5, bk=2048, name="ffn_down")


def _mixer(h, batch, tables, p, w, layer):
    rc, rs, mc, ma, mb = tables
    t, d = h.shape
    s = t // batch
    u = rmsnorm(h, p["mix_norm"][layer], BF16)
    za = matmul(u, w["w_in_a"], layer, out_dtype=F32, name="in_proj_f32")
    zb = matmul(u, w["w_in_b"], layer, col=w["w_in_b_col"], n=B_WIDTH, out_dtype=BF16, name="in_proj_bf16")
    za3 = za.reshape(batch, s, A_WIDTH)
    zb3 = zb.reshape(batch, s, B_WIDTH)
    tab3 = lambda x: x.reshape(batch, s, LANES)

    ro = retention(za3, zb3, tab3(rc), tab3(rs), p["ret_norm"][layer])
    fo = attention(zb3, B_FQ, zb3, B_FK, zb3, B_FV, heads=FOX_HEADS, dk=FOX_DH, dv=FOX_DH,
                   aux=forget_bias_lanes(za3, p["b_forget"][layer]), chunked_mask=False)
    qm = mla_queries(za, p["mla_q_norm"][layer], w["w_uq"], layer, mc, ma, mb)
    km, vm = mla_keys_values(za, p["mla_kv_norm"][layer], w["w_uk"], w["w_uv"], layer, mc, ma, mb)
    mo = attention(qm.reshape(batch, s, -1), 0, km.reshape(batch, s, -1), 0, vm.reshape(batch, s, -1), 0,
                   heads=MLA_HEADS, dk=MLA_QK_PAD, dv=MLA_DV, chunked_mask=True)

    merged = gated_merge(u, ro.reshape(t, -1), fo.reshape(t, -1), mo.reshape(t, -1), w["w_gate"],
                         p["b_gate"], w["w_up_ret"], w["w_up_fox"], w["w_up_mla"], layer)
    return matmul(merged, w["w_out"], layer, res=h, scale=1.0, name="out_proj")


def kernel(x, positions, ffn1_norm, ffn1_w13, ffn1_w2, mix_norm, w_in, b_forget, ret_norm, mla_q_norm,
           mla_kv_norm, w_uq, w_ukv, w_up_ret, w_up_fox, w_up_mla, w_gate, b_gate, w_out, ffn2_norm,
           ffn2_w13, ffn2_w2, final_norm):
    p = dict(mix_norm=mix_norm, w_in=w_in, b_forget=b_forget, ret_norm=ret_norm, mla_q_norm=mla_q_norm,
             mla_kv_norm=mla_kv_norm, w_uq=w_uq, w_ukv=w_ukv, w_up_ret=w_up_ret, w_up_fox=w_up_fox,
             w_up_mla=w_up_mla, w_gate=w_gate, b_gate=b_gate, w_out=w_out, ffn1_w13=ffn1_w13,
             ffn1_w2=ffn1_w2, ffn2_w13=ffn2_w13, ffn2_w2=ffn2_w2)
    batch, s, d = x.shape
    h = x.reshape(batch * s, d)
    w = _prepare_weights(p)
    tables = rope_tables(positions)
    for layer in range(ffn1_norm.shape[0]):
        h = _ffn(h, ffn1_norm[layer], w["ffn1_w13"], w["ffn1_w2"], layer)
        h = _mixer(h, batch, tables, p, w, layer)
        h = _ffn(h, ffn2_norm[layer], w["ffn2_w13"], w["ffn2_w2"], layer)
    return rmsnorm(h, final_norm, x.dtype).reshape(batch, s, d)
```

```python
import functools
import math

import jax
import jax.numpy as jnp
from jax import lax
from jax.experimental import pallas as pl
from jax.experimental.pallas import tpu as pltpu

F32 = jnp.float32
BF16 = jnp.bfloat16

CHUNK = 64
RET_HEADS, RET_DK, RET_DV = 8, 128, 256
FOX_HEADS, FOX_DH = 8, 128
MLA_HEADS, MLA_Q_LORA, MLA_KV_LORA = 8, 1024, 512
MLA_NOPE, MLA_ROPE, MLA_DV = 128, 64, 128
MLA_QK_PAD = 256
ROPE_BASE = 10000.0
EPS = 1e-6
NEG_BIG = -1e30
LOG2_E = math.log2(math.e)
N_SPLIT = 3
ATTN_HEADS_PER_STEP = 8

LANES = 128
SUBLANES = 8
VMEM_LIMIT_BYTES = 56 * 1024 * 1024

A_RQ, A_RK, A_CQ, A_CKV, A_KR = 0, 1024, 2048, 3072, 3584
A_FF = A_KR + MLA_ROPE
A_WIDTH = 4096
FF_LANE = A_FF - A_KR
B_RV, B_RG, B_FQ, B_FK, B_FV = 0, 2048, 4096, 5120, 6144
B_WIDTH = 7168


def _tile(dim, pref, quantum=LANES):
    if dim <= pref:
        return dim
    t = (pref // quantum) * quantum
    while t > quantum and dim % t:
        t -= quantum
    assert dim % t == 0, (dim, pref)
    return t


def _params(*sem):
    return pltpu.CompilerParams(dimension_semantics=sem, vmem_limit_bytes=VMEM_LIMIT_BYTES)


def _dot(a, b):
    return jnp.dot(a, b, preferred_element_type=F32)


def _dot_nt(a, b):
    return lax.dot_general(a, b, (((1,), (1,)), ((), ())), preferred_element_type=F32)


def _dot_tn(a, b):
    return lax.dot_general(a, b, (((0,), (0,)), ((), ())), preferred_element_type=F32)


def _rmsnorm_kernel(x_ref, g_ref, o_ref):
    x = x_ref[...]
    ms = jnp.mean(x * x, axis=-1, keepdims=True)
    o_ref[...] = (x * lax.rsqrt(ms + EPS) * g_ref[...]).astype(o_ref.dtype)


def rmsnorm(x, g, out_dtype):
    m, d = x.shape
    bm = _tile(m, 512, SUBLANES)
    return pl.pallas_call(
        _rmsnorm_kernel,
        grid=(m // bm,),
        in_specs=[pl.BlockSpec((bm, d), lambda i: (i, 0)),
                  pl.BlockSpec((1, d), lambda i: (0, 0))],
        out_specs=pl.BlockSpec((bm, d), lambda i: (i, 0)),
        out_shape=jax.ShapeDtypeStruct((m, d), out_dtype),
        compiler_params=_params("parallel"),
        name="rmsnorm",
    )(x, g.reshape(1, d))


def _row_rstd(ssq_ref, inv_d, width):
    rstd = lax.rsqrt(ssq_ref[...] * inv_d + EPS)
    return jnp.concatenate([rstd] * (width // LANES), axis=1)


def _prenorm_kernel(x_ref, g_ref, u_ref, ssq_ref):
    x = x_ref[...]
    u_ref[...] = (x * g_ref[...]).astype(u_ref.dtype)
    ssq_ref[...] = jnp.broadcast_to(jnp.sum(x * x, axis=-1, keepdims=True), ssq_ref.shape)


def prenorm(x, g):
    m, d = x.shape
    bm = _tile(m, 512, SUBLANES)
    return pl.pallas_call(
        _prenorm_kernel,
        grid=(m // bm,),
        in_specs=[pl.BlockSpec((bm, d), lambda i: (i, 0)),
                  pl.BlockSpec((1, d), lambda i: (0, 0))],
        out_specs=[pl.BlockSpec((bm, d), lambda i: (i, 0)),
                   pl.BlockSpec((bm, LANES), lambda i: (i, 0))],
        out_shape=[jax.ShapeDtypeStruct((m, d), BF16), jax.ShapeDtypeStruct((m, LANES), F32)],
        compiler_params=_params("parallel"),
        name="prenorm",
    )(x, g.reshape(1, d))


def _proj_kernel(a_ref, ssq_ref, w_ref, o_ref, *, inv_d):
    acc = _dot(a_ref[...], w_ref[...])
    o_ref[...] = (acc * _row_rstd(ssq_ref, inv_d, acc.shape[1])).astype(o_ref.dtype)


def normed_matmul(a, ssq, w, layer, *, col=0, n=None, out_dtype, bm=1024, bn=1024, name):
    m, kd = a.shape
    n = w.shape[2] if n is None else n
    bm, bn = _tile(m, bm, SUBLANES), _tile(math.gcd(n, col) if col else n, bn)
    first = col // bn
    return pl.pallas_call(
        functools.partial(_proj_kernel, inv_d=1.0 / kd),
        grid=(m // bm, n // bn),
        in_specs=[pl.BlockSpec((bm, kd), lambda i, j: (i, 0)),
                  pl.BlockSpec((bm, LANES), lambda i, j: (i, 0)),
                  pl.BlockSpec((None, kd, bn), lambda i, j: (layer, 0, first + j))],
        out_specs=pl.BlockSpec((bm, bn), lambda i, j: (i, j)),
        out_shape=jax.ShapeDtypeStruct((m, n), out_dtype),
        compiler_params=_params("parallel", "parallel"),
        name=name,
    )(a, ssq, w)


def _residual_kernel(a_ref, w_ref, r_ref, *refs, nk, scale, emit_norm):
    if emit_norm:
        g_ref, o_ref, u_ref, ssq_ref, *scratch = refs
    else:
        o_ref, *scratch = refs

    def finish(acc):
        h = r_ref[...] + scale * acc
        o_ref[...] = h
        if emit_norm:
            u_ref[...] = (h * g_ref[...]).astype(u_ref.dtype)
            row_ssq = jnp.broadcast_to(jnp.sum(h * h, axis=1, keepdims=True), ssq_ref.shape)
            j = pl.program_id(1)

            @pl.when(j == 0)
            def _():
                ssq_ref[...] = row_ssq

            @pl.when(j > 0)
            def _():
                ssq_ref[...] += row_ssq

    part = _dot(a_ref[...], w_ref[...])
    if nk == 1:
        finish(part)
        return
    acc_ref, = scratch
    k = pl.program_id(2)

    @pl.when(k == 0)
    def _():
        acc_ref[...] = part

    @pl.when(jnp.logical_and(k > 0, k < nk - 1))
    def _():
        acc_ref[...] += part

    @pl.when(k == nk - 1)
    def _():
        finish(acc_ref[...] + part)


def residual_matmul(a, w, layer, res, *, scale, gain=None, bm=1024, bn=1024, bk=None, name):
    m, kd = a.shape
    n = w.shape[2]
    bm, bn = _tile(m, bm, SUBLANES), _tile(n, bn)
    bk = kd if bk is None else _tile(kd, bk)
    nk = kd // bk
    tile = pl.BlockSpec((bm, bn), lambda i, j, k: (i, j))
    in_specs = [pl.BlockSpec((bm, bk), lambda i, j, k: (i, k)),
                pl.BlockSpec((None, bk, bn), lambda i, j, k: (layer, k, j)),
                tile]
    args = [a, w, res]
    out_specs, out_shape = tile, jax.ShapeDtypeStruct((m, n), F32)
    if gain is not None:
        in_specs.append(pl.BlockSpec((1, bn), lambda i, j, k: (0, j)))
        args.append(gain.reshape(1, n))
        out_specs = [tile, tile, pl.BlockSpec((bm, LANES), lambda i, j, k: (i, 0))]
        out_shape = [out_shape, jax.ShapeDtypeStruct((m, n), BF16), jax.ShapeDtypeStruct((m, LANES), F32)]
    return pl.pallas_call(
        functools.partial(_residual_kernel, nk=nk, scale=scale, emit_norm=gain is not None),
        grid=(m // bm, n // bn, nk),
        in_specs=in_specs,
        out_specs=out_specs,
        out_shape=out_shape,
        scratch_shapes=[pltpu.VMEM((bm, bn), F32)] if nk > 1 else [],
        compiler_params=_params("parallel", "arbitrary", "arbitrary"),
        name=name,
    )(*args)


def _swiglu_kernel(u_ref, ssq_ref, w1_ref, w3_ref, o_ref, *, inv_d):
    u = u_ref[...]
    rstd = _row_rstd(ssq_ref, inv_d, o_ref.shape[1])
    a = _dot(u, w1_ref[...]) * rstd
    b = _dot(u, w3_ref[...]) * rstd
    o_ref[...] = (a * jax.nn.sigmoid(a) * b).astype(o_ref.dtype)


def swiglu_up(u, ssq, w13, layer, *, bm=1024, bn=512):
    m, d = u.shape
    f = w13.shape[2] // 2
    bm, bn = _tile(m, bm, SUBLANES), _tile(f, bn)
    nf = f // bn
    return pl.pallas_call(
        functools.partial(_swiglu_kernel, inv_d=1.0 / d),
        grid=(m // bm, nf),
        in_specs=[pl.BlockSpec((bm, d), lambda i, j: (i, 0)),
                  pl.BlockSpec((bm, LANES), lambda i, j: (i, 0)),
                  pl.BlockSpec((None, d, bn), lambda i, j: (layer, 0, j)),
                  pl.BlockSpec((None, d, bn), lambda i, j: (layer, 0, j + nf))],
        out_specs=pl.BlockSpec((bm, bn), lambda i, j: (i, j)),
        out_shape=jax.ShapeDtypeStruct((m, f), BF16),
        compiler_params=_params("parallel", "parallel"),
        name="swiglu_up",
    )(u, ssq, w13, w13)


def _merge_kernel(u_ref, ssq_ref, ro_ref, fo_ref, mo_ref, wg_ref, bg_ref, wr_ref, wf_ref, wm_ref, o_ref, *,
                  inv_d):
    u = u_ref[...]
    rstd = _row_rstd(ssq_ref, inv_d, o_ref.shape[1])
    total = None
    for i, (x_ref, w_ref) in enumerate(((ro_ref, wr_ref), (fo_ref, wf_ref), (mo_ref, wm_ref))):
        gate = jax.nn.sigmoid(_dot(u, wg_ref[i]) * rstd + bg_ref[i:i + 1, :])
        term = gate * _dot(x_ref[...], w_ref[...])
        total = term if total is None else total + term
    o_ref[...] = total.astype(o_ref.dtype)


def gated_merge(u, ssq, ro, fo, mo, w_gate, b_gate, w_up_ret, w_up_fox, w_up_mla, layer, *, bm=512, bn=256):
    m, d = u.shape
    n = w_gate.shape[-1]
    bm, bn = _tile(m, bm, SUBLANES), _tile(n, bn)
    row = lambda width: pl.BlockSpec((bm, width), lambda i, j: (i, 0))
    col = lambda depth: pl.BlockSpec((None, depth, bn), lambda i, j: (layer, 0, j))
    return pl.pallas_call(
        functools.partial(_merge_kernel, inv_d=1.0 / d),
        grid=(m // bm, n // bn),
        in_specs=[row(d), row(LANES), row(ro.shape[1]), row(fo.shape[1]), row(mo.shape[1]),
                  pl.BlockSpec((None, 3, d, bn), lambda i, j: (layer, 0, 0, j)),
                  pl.BlockSpec((None, 3, bn), lambda i, j: (layer, 0, j)),
                  col(ro.shape[1]), col(fo.shape[1]), col(mo.shape[1])],
        out_specs=pl.BlockSpec((bm, bn), lambda i, j: (i, j)),
        out_shape=jax.ShapeDtypeStruct((m, n), BF16),
        compiler_params=_params("parallel", "parallel"),
        name="gated_merge",
    )(u, ssq, ro, fo, mo, w_gate, b_gate, w_up_ret, w_up_fox, w_up_mla)


def _rope_table_kernel(pos_ref, inv_ref, coef_ref, rc_ref, rs_ref, mc_ref, ma_ref, mb_ref):
    p = pos_ref[...].astype(F32)
    ang_r = p * inv_ref[0:1, :]
    ang_m = p * inv_ref[1:2, :]
    rc_ref[...] = jnp.cos(ang_r)
    rs_ref[...] = jnp.sin(ang_r) * coef_ref[0:1, :]
    sin_m = jnp.sin(ang_m)
    mc_ref[...] = jnp.cos(ang_m) * coef_ref[1:2, :]
    ma_ref[...] = sin_m * coef_ref[2:3, :]
    mb_ref[...] = sin_m * coef_ref[3:4, :]


def rope_tables(positions):
    t = positions.size
    h_r, h_m = RET_DK // 2, MLA_ROPE // 2
    inv_r = ROPE_BASE ** (-jnp.arange(h_r, dtype=F32) / h_r)
    inv_m = ROPE_BASE ** (-jnp.arange(h_m, dtype=F32) / h_m)
    zeros_m = jnp.zeros((LANES - 2 * h_m,), F32)
    ones_m = jnp.ones((h_m,), F32)
    inv = jnp.stack([jnp.concatenate([inv_r, inv_r]),
                     jnp.concatenate([inv_m, inv_m, zeros_m])])
    coef = jnp.stack([
        jnp.concatenate([-jnp.ones((h_r,), F32), jnp.ones((h_r,), F32)]),
        jnp.concatenate([ones_m, ones_m, zeros_m]),
        jnp.concatenate([-ones_m, 0 * ones_m, zeros_m]),
        jnp.concatenate([0 * ones_m, ones_m, zeros_m]),
    ])
    bm = _tile(t, 1024, SUBLANES)
    tab = jax.ShapeDtypeStruct((t, LANES), F32)
    return pl.pallas_call(
        _rope_table_kernel,
        grid=(t // bm,),
        in_specs=[pl.BlockSpec((bm, 1), lambda i: (i, 0)),
                  pl.BlockSpec((2, LANES), lambda i: (0, 0)),
                  pl.BlockSpec((4, LANES), lambda i: (0, 0))],
        out_specs=[pl.BlockSpec((bm, LANES), lambda i: (i, 0))] * 5,
        out_shape=[tab] * 5,
        compiler_params=_params("parallel"),
        name="rope_tables",
    )(positions.reshape(t, 1), inv, coef)


def _ret_log_gamma(h):
    return math.log1p(-2.0 ** (-5.0 - h))


def _retention_kernel(q_ref, k_ref, v_ref, g_ref, rc_ref, rs_ref, norm_ref, o_ref, state_ref, decay_ref,
                      *, blk):
    first = jnp.logical_and(pl.program_id(0) == 0, pl.program_id(1) == 0)

    @pl.when(first)
    def _():
        row = lax.broadcasted_iota(jnp.int32, (blk, blk), 0)
        col = lax.broadcasted_iota(jnp.int32, (blk, blk), 1)
        dist = jnp.abs(row - col).astype(F32)
        visible = (col // CHUNK) <= (row // CHUNK)
        for h in range(RET_HEADS):
            decay_ref[h] = jnp.where(visible, jnp.exp(_ret_log_gamma(h) * dist), 0.0)

    @pl.when(pl.program_id(1) == 0)
    def _():
        state_ref[...] = jnp.zeros_like(state_ref)

    rc = rc_ref[0]
    rs = rs_ref[0]
    idx = lax.broadcasted_iota(jnp.int32, (blk, 1), 0).astype(F32)
    for h in range(RET_HEADS):
        lg = _ret_log_gamma(h)
        qk = slice(h * RET_DK, (h + 1) * RET_DK)
        vv = slice(h * RET_DV, (h + 1) * RET_DV)
        q = q_ref[0, :, qk]
        k = k_ref[0, :, qk]
        q = (q * rc + pltpu.roll(q, RET_DK // 2, 1) * rs) * (RET_DK ** -0.5)
        k = k * rc + pltpu.roll(k, RET_DK // 2, 1) * rs
        v = v_ref[0, :, vv]
        state = state_ref[h]
        scores = _dot_nt(q.astype(BF16), k.astype(BF16)) * decay_ref[h]
        q_in = (q * jnp.exp(lg * (idx + 1.0))).astype(BF16)
        out = _dot(scores.astype(BF16), v) + _dot(q_in, state.astype(BF16))
        k_out = (k * jnp.exp(lg * (blk - 1.0 - idx))).astype(BF16)
        state_ref[h] = state * math.exp(lg * blk) + _dot_tn(k_out, v)
        out = out * lax.rsqrt(jnp.mean(out * out, axis=-1, keepdims=True) + EPS)
        gate = g_ref[0, :, vv].astype(F32)
        out = out * norm_ref[:, vv] * (gate * jax.nn.sigmoid(gate))
        o_ref[0, :, vv] = out.astype(o_ref.dtype)


def retention(za, zb, rc, rs, ret_norm, *, blk=256):
    b, s, _ = za.shape
    blk = _tile(s, blk, CHUNK)
    qw, vw = RET_HEADS * RET_DK, RET_HEADS * RET_DV
    return pl.pallas_call(
        functools.partial(_retention_kernel, blk=blk),
        grid=(b, s // blk),
        in_specs=[pl.BlockSpec((1, blk, qw), lambda i, j: (i, j, A_RQ // qw)),
                  pl.BlockSpec((1, blk, qw), lambda i, j: (i, j, A_RK // qw)),
                  pl.BlockSpec((1, blk, vw), lambda i, j: (i, j, B_RV // vw)),
                  pl.BlockSpec((1, blk, vw), lambda i, j: (i, j, B_RG // vw)),
                  pl.BlockSpec((1, blk, LANES), lambda i, j: (i, j, 0)),
                  pl.BlockSpec((1, blk, LANES), lambda i, j: (i, j, 0)),
                  pl.BlockSpec((1, vw), lambda i, j: (0, 0))],
        out_specs=pl.BlockSpec((1, blk, vw), lambda i, j: (i, j, 0)),
        out_shape=jax.ShapeDtypeStruct((b, s, vw), BF16),
        scratch_shapes=[pltpu.VMEM((RET_HEADS, RET_DK, RET_DV), F32),
                        pltpu.VMEM((RET_HEADS, blk, blk), F32)],
        compiler_params=_params("arbitrary", "arbitrary"),
        name="retention",
    )(za, za, zb, zb, rc, rs, ret_norm.reshape(1, vw))


def _split3(x):
    hi = x.astype(BF16).astype(F32)
    rest = x - hi
    mid = rest.astype(BF16).astype(F32)
    lo = (rest - mid).astype(BF16).astype(F32)
    return hi, mid, lo


def _forget_cumsum_kernel(z_ref, bias_ref, qa_ref, ka_ref, carry_ref, *, rows):
    @pl.when(pl.program_id(1) == 0)
    def _():
        carry_ref[...] = jnp.zeros_like(carry_ref)

    lane = lax.broadcasted_iota(jnp.int32, (rows, LANES), 1)
    is_gate = jnp.logical_and(lane >= FF_LANE, lane < FF_LANE + FOX_HEADS)
    x = z_ref[0] + bias_ref[...]
    log_f = jnp.minimum(x, 0.0) - jnp.log(1.0 + jnp.exp(-jnp.abs(x)))
    log_f = jnp.where(is_gate, log_f * LOG2_E, 0.0)
    tri = (lax.broadcasted_iota(jnp.int32, (rows, rows), 0)
           >= lax.broadcasted_iota(jnp.int32, (rows, rows), 1)).astype(BF16)
    cum = carry_ref[...]
    for piece in _split3(log_f):
        cum = cum + _dot(tri, piece.astype(BF16))
    carry_ref[...] = cum[rows - 1:rows, :]
    one = jnp.where(lane < 2 * N_SPLIT, 1.0, 0.0)
    for h in range(FOX_HEADS):
        col = jnp.sum(jnp.where(lane == FF_LANE + h, cum, 0.0), axis=1, keepdims=True)
        pieces = _split3(jnp.broadcast_to(col, (rows, LANES)))
        qa = jnp.where(lane < N_SPLIT, 0.0, one)
        ka = jnp.where(lane < N_SPLIT, one, 0.0)
        for i, piece in enumerate(pieces):
            qa = jnp.where(lane == i, piece, qa)
            ka = jnp.where(lane == N_SPLIT + i, -piece, ka)
        qa_ref[0, :, h * LANES:(h + 1) * LANES] = qa.astype(qa_ref.dtype)
        ka_ref[0, :, h * LANES:(h + 1) * LANES] = ka.astype(ka_ref.dtype)


def forget_bias_lanes(za, b_forget, *, rows=256):
    b, s, _ = za.shape
    rows = _tile(s, rows)
    bias = jnp.zeros((1, LANES), F32).at[0, FF_LANE:FF_LANE + FOX_HEADS].set(b_forget)
    aux = jax.ShapeDtypeStruct((b, s, FOX_HEADS * LANES), BF16)
    return pl.pallas_call(
        functools.partial(_forget_cumsum_kernel, rows=rows),
        grid=(b, s // rows),
        in_specs=[pl.BlockSpec((1, rows, LANES), lambda i, j: (i, j, A_KR // LANES)),
                  pl.BlockSpec((1, LANES), lambda i, j: (0, 0))],
        out_specs=[pl.BlockSpec((1, rows, FOX_HEADS * LANES), lambda i, j: (i, j, 0))] * 2,
        out_shape=[aux, aux],
        scratch_shapes=[pltpu.VMEM((1, LANES), F32)],
        compiler_params=_params("arbitrary", "arbitrary"),
        name="forget_bias_lanes",
    )(za, bias)


def _attn_kernel(qi_ref, ki_ref, *refs, has_aux, chunked_mask, hp, dk, dv, t):
    if has_aux:
        q_ref, qa_ref, k_ref, ka_ref, v_ref, o_ref, m_ref, l_ref, acc_ref = refs
    else:
        q_ref, k_ref, v_ref, o_ref, m_ref, l_ref, acc_ref = refs
    pair = pl.program_id(2)
    qi, ki = qi_ref[pair], ki_ref[pair]

    @pl.when(ki == 0)
    def _():
        m_ref[...] = jnp.full_like(m_ref, NEG_BIG)
        l_ref[...] = jnp.zeros_like(l_ref)
        acc_ref[...] = jnp.zeros_like(acc_ref)

    def step(diagonal):
        if diagonal:
            row = lax.broadcasted_iota(jnp.int32, (t, t), 0)
            col = lax.broadcasted_iota(jnp.int32, (t, t), 1)
            keep = (col // CHUNK) <= (row // CHUNK) if chunked_mask else col <= row
        for h in range(hp):
            q = q_ref[0, :, h * dk:(h + 1) * dk]
            k = k_ref[0, :, h * dk:(h + 1) * dk]
            if has_aux:
                q = jnp.concatenate([q, qa_ref[0, :, h * LANES:(h + 1) * LANES]], axis=1)
                k = jnp.concatenate([k, ka_ref[0, :, h * LANES:(h + 1) * LANES]], axis=1)
            s = _dot_nt(q, k)
            if diagonal:
                s = jnp.where(keep, s, NEG_BIG)
            m_old = m_ref[h]
            m_new = jnp.maximum(m_old, jnp.max(s, axis=1, keepdims=True))
            alpha = jnp.exp2(m_old - m_new)
            p = jnp.exp2(s - jnp.concatenate([m_new] * (t // LANES), axis=1))
            l_ref[h] = alpha * l_ref[h] + jnp.sum(p, axis=1, keepdims=True)
            acc_ref[h] = alpha * acc_ref[h] + _dot(p.astype(BF16), v_ref[0, :, h * dv:(h + 1) * dv])
            m_ref[h] = m_new

    @pl.when(ki < qi)
    def _():
        step(False)

    @pl.when(ki == qi)
    def _():
        step(True)
        for h in range(hp):
            o_ref[0, :, h * dv:(h + 1) * dv] = (acc_ref[h] / l_ref[h]).astype(o_ref.dtype)


def attention(q_arr, q_col, k_arr, k_col, v_arr, v_col, *, heads, dk, dv, aux=None, chunked_mask,
              tile=512, hp=ATTN_HEADS_PER_STEP):
    assert dv == LANES and heads % hp == 0
    b, s, _ = q_arr.shape
    t = _tile(s, tile)
    n = s // t
    pairs = [(i, j) for i in range(n) for j in range(i + 1)]
    qi_tab = jnp.asarray([p[0] for p in pairs], jnp.int32)
    ki_tab = jnp.asarray([p[1] for p in pairs], jnp.int32)

    def spec(width, col, table):
        assert col % (hp * width) == 0
        first = col // (hp * width)
        if table == "q":
            return pl.BlockSpec((1, t, hp * width), lambda i, g, p, qt, kt: (i, qt[p], first + g))
        return pl.BlockSpec((1, t, hp * width), lambda i, g, p, qt, kt: (i, kt[p], first + g))

    if aux is None:
        in_specs = [spec(dk, q_col, "q"), spec(dk, k_col, "k"), spec(dv, v_col, "k")]
        args = [q_arr, k_arr, v_arr]
    else:
        in_specs = [spec(dk, q_col, "q"), spec(LANES, 0, "q"), spec(dk, k_col, "k"), spec(LANES, 0, "k"),
                    spec(dv, v_col, "k")]
        args = [q_arr, aux[0], k_arr, aux[1], v_arr]
    stat = pltpu.VMEM((hp, t, LANES), F32)
    return pl.pallas_call(
        functools.partial(_attn_kernel, has_aux=aux is not None, chunked_mask=chunked_mask, hp=hp, dk=dk,
                          dv=dv, t=t),
        grid_spec=pltpu.PrefetchScalarGridSpec(
            num_scalar_prefetch=2,
            grid=(b, heads // hp, len(pairs)),
            in_specs=in_specs,
            out_specs=spec(dv, 0, "q"),
            scratch_shapes=[stat, stat, pltpu.VMEM((hp, t, dv), F32)]),
        out_shape=jax.ShapeDtypeStruct((b, s, heads * dv), BF16),
        compiler_params=_params("parallel", "parallel", "arbitrary"),
        name="fox_attention" if aux is not None else "mla_attention",
    )(qi_tab, ki_tab, *args)


def _mla_rope(x, mc, ma, mb):
    return x * mc + pltpu.roll(x, LANES - MLA_ROPE // 2, 1) * ma + pltpu.roll(x, MLA_ROPE // 2, 1) * mb


def _mla_q_kernel(c_ref, g_ref, w_ref, mc_ref, ma_ref, mb_ref, o_ref):
    x = c_ref[...]
    xn = (x * lax.rsqrt(jnp.mean(x * x, axis=-1, keepdims=True) + EPS) * g_ref[...]).astype(BF16)
    q = _dot(xn, w_ref[...])
    scale = (MLA_NOPE + MLA_ROPE) ** -0.5 * LOG2_E
    mc, ma, mb = mc_ref[...], ma_ref[...], mb_ref[...]
    for h in range(MLA_HEADS):
        lo = h * MLA_QK_PAD
        o_ref[:, lo:lo + MLA_NOPE] = (q[:, lo:lo + MLA_NOPE] * scale).astype(o_ref.dtype)
        pe = _mla_rope(q[:, lo + MLA_NOPE:lo + MLA_QK_PAD], mc, ma, mb)
        o_ref[:, lo + MLA_NOPE:lo + MLA_QK_PAD] = (pe * scale).astype(o_ref.dtype)


def mla_queries(za2, q_norm, w_uq_pad, layer, mc, ma, mb, *, bm=512):
    t = za2.shape[0]
    bm = _tile(t, bm, SUBLANES)
    tab = pl.BlockSpec((bm, LANES), lambda i: (i, 0))
    return pl.pallas_call(
        _mla_q_kernel,
        grid=(t // bm,),
        in_specs=[pl.BlockSpec((bm, MLA_Q_LORA), lambda i: (i, A_CQ // MLA_Q_LORA)),
                  pl.BlockSpec((1, MLA_Q_LORA), lambda i: (0, 0)),
                  pl.BlockSpec((None,) + w_uq_pad.shape[1:], lambda i: (layer, 0, 0)),
                  tab, tab, tab],
        out_specs=pl.BlockSpec((bm, MLA_HEADS * MLA_QK_PAD), lambda i: (i, 0)),
        out_shape=jax.ShapeDtypeStruct((t, MLA_HEADS * MLA_QK_PAD), BF16),
        compiler_params=_params("parallel"),
        name="mla_queries",
    )(za2, q_norm.reshape(1, -1), w_uq_pad, mc, ma, mb)


def _mla_kv_kernel(c_ref, kr_ref, g_ref, wk_ref, wv_ref, mc_ref, ma_ref, mb_ref, k_ref, v_ref):
    x = c_ref[...]
    xn = (x * lax.rsqrt(jnp.mean(x * x, axis=-1, keepdims=True) + EPS) * g_ref[...]).astype(BF16)
    k_nope = _dot(xn, wk_ref[...])
    v_ref[...] = _dot(xn, wv_ref[...]).astype(v_ref.dtype)
    k_pe = _mla_rope(kr_ref[...], mc_ref[...], ma_ref[...], mb_ref[...]).astype(k_ref.dtype)
    for h in range(MLA_HEADS):
        lo = h * MLA_QK_PAD
        k_ref[:, lo:lo + MLA_NOPE] = k_nope[:, h * MLA_NOPE:(h + 1) * MLA_NOPE].astype(k_ref.dtype)
        k_ref[:, lo + MLA_NOPE:lo + MLA_QK_PAD] = k_pe


def mla_keys_values(za2, kv_norm, wk, wv, layer, mc, ma, mb, *, bm=512):
    t = za2.shape[0]
    bm = _tile(t, bm, SUBLANES)
    tab = pl.BlockSpec((bm, LANES), lambda i: (i, 0))
    w_spec = pl.BlockSpec((None,) + wk.shape[1:], lambda i: (layer, 0, 0))
    return pl.pallas_call(
        _mla_kv_kernel,
        grid=(t // bm,),
        in_specs=[pl.BlockSpec((bm, MLA_KV_LORA), lambda i: (i, A_CKV // MLA_KV_LORA)),
                  pl.BlockSpec((bm, LANES), lambda i: (i, A_KR // LANES)),
                  pl.BlockSpec((1, MLA_KV_LORA), lambda i: (0, 0)),
                  w_spec, w_spec, tab, tab, tab],
        out_specs=[pl.BlockSpec((bm, MLA_HEADS * MLA_QK_PAD), lambda i: (i, 0)),
                   pl.BlockSpec((bm, MLA_HEADS * MLA_DV), lambda i: (i, 0))],
        out_shape=[jax.ShapeDtypeStruct((t, MLA_HEADS * MLA_QK_PAD), BF16),
                   jax.ShapeDtypeStruct((t, MLA_HEADS * MLA_DV), BF16)],
        compiler_params=_params("parallel"),
        name="mla_keys_values",
    )(za2, za2, kv_norm.reshape(1, -1), wk, wv, mc, ma, mb)


def _prepare_weights(p):
    w_in = p["w_in"]
    depth, d, width = w_in.shape
    rq_rk_end = 2 * RET_HEADS * RET_DK
    bf_end = rq_rk_end + B_WIDTH
    ff_end = bf_end + FOX_HEADS
    fq0 = rq_rk_end + B_FQ
    col_scale = jnp.ones((width,), F32).at[fq0:fq0 + FOX_HEADS * FOX_DH].set(FOX_DH ** -0.5 * LOG2_E)
    w_in_b =(w_in * col_scale).astype(BF16)
    used = rq_rk_end + (width - ff_end) + FOX_HEADS
    w_in_a = jnp.concatenate([w_in_b[:, :, :rq_rk_end], w_in_b[:, :, ff_end:], w_in_b[:, :, bf_end:ff_end],
                              jnp.zeros((depth, d, A_WIDTH - used), BF16)], axis=2)
    w_uq = p["w_uq"].reshape(depth, MLA_Q_LORA, MLA_HEADS, MLA_NOPE + MLA_ROPE)
    w_uq = jnp.pad(w_uq, ((0, 0), (0, 0), (0, 0), (0, MLA_QK_PAD - MLA_NOPE - MLA_ROPE)))
    w_ukv = p["w_ukv"].reshape(depth, MLA_KV_LORA, MLA_HEADS, MLA_NOPE + MLA_DV)
    out = {k: p[k].astype(BF16) for k in ("ffn1_w13", "ffn1_w2", "ffn2_w13", "ffn2_w2", "w_gate", "w_out",
                                          "w_up_ret", "w_up_fox", "w_up_mla")}
    out.update(
        w_in_a=w_in_a, w_in_b=w_in_b, w_in_b_col=rq_rk_end,
        w_uq=w_uq.reshape(depth, MLA_Q_LORA, MLA_HEADS * MLA_QK_PAD).astype(BF16),
        w_uk=w_ukv[..., :MLA_NOPE].reshape(depth, MLA_KV_LORA, MLA_HEADS * MLA_NOPE).astype(BF16),
        w_uv=w_ukv[..., MLA_NOPE:].reshape(depth, MLA_KV_LORA, MLA_HEADS * MLA_DV).astype(BF16))
    return out


def _ffn(h, u, ssq, w13, w2, layer, next_gain):
    g = swiglu_up(u, ssq, w13, layer)
    return residual_matmul(g, w2, layer, h, scale=0.5, gain=next_gain, bk=2048, name="ffn_down")


def _mixer(h, u, ssq, batch, tables, p, w, layer, next_gain):
    rc, rs, mc, ma, mb = tables
    t, d = h.shape
    s = t // batch
    za = normed_matmul(u, ssq, w["w_in_a"], layer, out_dtype=F32, name="in_proj_f32")
    zb = normed_matmul(u, ssq, w["w_in_b"], layer, col=w["w_in_b_col"], n=B_WIDTH, out_dtype=BF16,
                       name="in_proj_bf16")
    za3 = za.reshape(batch, s, A_WIDTH)
    zb3 = zb.reshape(batch, s, B_WIDTH)
    tab3 = lambda x: x.reshape(batch, s, LANES)

    ro = retention(za3, zb3, tab3(rc), tab3(rs), p["ret_norm"][layer])
    fo = attention(zb3, B_FQ, zb3, B_FK, zb3, B_FV, heads=FOX_HEADS, dk=FOX_DH, dv=FOX_DH,
                   aux=forget_bias_lanes(za3, p["b_forget"][layer]), chunked_mask=False)
    qm = mla_queries(za, p["mla_q_norm"][layer], w["w_uq"], layer, mc, ma, mb)
    km, vm = mla_keys_values(za, p["mla_kv_norm"][layer], w["w_uk"], w["w_uv"], layer, mc, ma, mb)
    mo = attention(qm.reshape(batch, s, -1), 0, km.reshape(batch, s, -1), 0, vm.reshape(batch, s, -1), 0,
                   heads=MLA_HEADS, dk=MLA_QK_PAD, dv=MLA_DV, chunked_mask=True)

    merged = gated_merge(u, ssq, ro.reshape(t, -1), fo.reshape(t, -1), mo.reshape(t, -1), w["w_gate"],
                         p["b_gate"], w["w_up_ret"], w["w_up_fox"], w["w_up_mla"], layer)
    return residual_matmul(merged, w["w_out"], layer, h, scale=1.0, gain=next_gain, bn=512, name="out_proj")


def kernel(x, positions, ffn1_norm, ffn1_w13, ffn1_w2, mix_norm, w_in, b_forget, ret_norm, mla_q_norm,
           mla_kv_norm, w_uq, w_ukv, w_up_ret, w_up_fox, w_up_mla, w_gate, b_gate, w_out, ffn2_norm,
           ffn2_w13, ffn2_w2, final_norm):
    p = dict(mix_norm=mix_norm, w_in=w_in, b_forget=b_forget, ret_norm=ret_norm, mla_q_norm=mla_q_norm,
             mla_kv_norm=mla_kv_norm, w_uq=w_uq, w_ukv=w_ukv, w_up_ret=w_up_ret, w_up_fox=w_up_fox,
             w_up_mla=w_up_mla, w_gate=w_gate, b_gate=b_gate, w_out=w_out, ffn1_w13=ffn1_w13,
             ffn1_w2=ffn1_w2, ffn2_w13=ffn2_w13, ffn2_w2=ffn2_w2)
    batch, s, d = x.shape
    h = x.reshape(batch * s, d)
    w = _prepare_weights(p)
    tables = rope_tables(positions)
    depth = ffn1_norm.shape[0]
    u, ssq = prenorm(h, ffn1_norm[0])
    for layer in range(depth):
        h, u, ssq = _ffn(h, u, ssq, w["ffn1_w13"], w["ffn1_w2"], layer, mix_norm[layer])
        h, u, ssq = _mixer(h, u, ssq, batch, tables, p, w, layer, ffn2_norm[layer])
        if layer + 1 < depth:
            h, u, ssq = _ffn(h, u, ssq, w["ffn2_w13"], w["ffn2_w2"], layer, ffn1_norm[layer + 1])
        else:
            h = _ffn(h, u, ssq, w["ffn2_w13"], w["ffn2_w2"], layer, None)
    return rmsnorm(h, final_norm, x.dtype).reshape(batch, s, d)
```

```python
import functools
import math

import jax
import jax.numpy as jnp
from jax import lax
from jax.experimental import pallas as pl
from jax.experimental.pallas import tpu as pltpu

F32 = jnp.float32
BF16 = jnp.bfloat16

CHUNK = 64
RET_HEADS, RET_DK, RET_DV = 8, 128, 256
FOX_HEADS, FOX_DH = 8, 128
MLA_HEADS, MLA_Q_LORA, MLA_KV_LORA = 8, 1024, 512
MLA_NOPE, MLA_ROPE, MLA_DV = 128, 64, 128
MLA_QK_PAD = 256
ROPE_BASE = 10000.0
EPS = 1e-6
NEG_BIG = -1e30
LOG2_E = math.log2(math.e)
N_SPLIT = 3
ATTN_HEADS_PER_STEP = 8

LANES = 128
SUBLANES = 8
VMEM_LIMIT_BYTES = 56 * 1024 * 1024

A_RQ, A_RK, A_CQ, A_CKV, A_KR = 0, 1024, 2048, 3072, 3584
A_FF = A_KR + MLA_ROPE
A_WIDTH = 4096
FF_LANE = A_FF - A_KR
B_RV, B_RG, B_FQ, B_FK, B_FV = 0, 2048, 4096, 5120, 6144
B_WIDTH = 7168


def _tile(dim, pref, quantum=LANES):
    if dim <= pref:
        return dim
    t = (pref // quantum) * quantum
    while t > quantum and dim % t:
        t -= quantum
    assert dim % t == 0, (dim, pref)
    return t


def _params(*sem):
    return pltpu.CompilerParams(dimension_semantics=sem, vmem_limit_bytes=VMEM_LIMIT_BYTES)


def _dot(a, b):
    return jnp.dot(a, b, preferred_element_type=F32)


def _dot_nt(a, b):
    return lax.dot_general(a, b, (((1,), (1,)), ((), ())), preferred_element_type=F32)


def _dot_tn(a, b):
    return lax.dot_general(a, b, (((0,), (0,)), ((), ())), preferred_element_type=F32)


def _rmsnorm_kernel(x_ref, g_ref, o_ref):
    x = x_ref[...]
    ms = jnp.mean(x * x, axis=-1, keepdims=True)
    o_ref[...] = (x * lax.rsqrt(ms + EPS) * g_ref[...]).astype(o_ref.dtype)


def rmsnorm(x, g, out_dtype):
    m, d = x.shape
    bm = _tile(m, 512, SUBLANES)
    return pl.pallas_call(
        _rmsnorm_kernel,
        grid=(m // bm,),
        in_specs=[pl.BlockSpec((bm, d), lambda i: (i, 0)),
                  pl.BlockSpec((1, d), lambda i: (0, 0))],
        out_specs=pl.BlockSpec((bm, d), lambda i: (i, 0)),
        out_shape=jax.ShapeDtypeStruct((m, d), out_dtype),
        compiler_params=_params("parallel"),
        name="rmsnorm",
    )(x, g.reshape(1, d))


def _lane_partial_ssq(x):
    sq = x * x
    part = sq[:, :LANES]
    for c in range(1, x.shape[1] // LANES):
        part = part + sq[:, c * LANES:(c + 1) * LANES]
    return part


def _row_rstd(ssq_ref, inv_d):
    return lax.rsqrt(jnp.sum(ssq_ref[...], axis=1, keepdims=True) * inv_d + EPS)


def _prenorm_kernel(x_ref, g_ref, u_ref, ssq_ref):
    x = x_ref[...]
    u_ref[...] = (x * g_ref[...]).astype(u_ref.dtype)
    ssq_ref[...] = _lane_partial_ssq(x)


def prenorm(x, g):
    m, d = x.shape
    bm = _tile(m, 512, SUBLANES)
    return pl.pallas_call(
        _prenorm_kernel,
        grid=(m // bm,),
        in_specs=[pl.BlockSpec((bm, d), lambda i: (i, 0)),
                  pl.BlockSpec((1, d), lambda i: (0, 0))],
        out_specs=[pl.BlockSpec((bm, d), lambda i: (i, 0)),
                   pl.BlockSpec((bm, LANES), lambda i: (i, 0))],
        out_shape=[jax.ShapeDtypeStruct((m, d), BF16), jax.ShapeDtypeStruct((m, LANES), F32)],
        compiler_params=_params("parallel"),
        name="prenorm",
    )(x, g.reshape(1, d))


def _proj_kernel(a_ref, ssq_ref, wt_ref, o_ref, *, inv_d):
    acc = _dot_nt(a_ref[...], wt_ref[...])
    o_ref[...] = (acc * _row_rstd(ssq_ref, inv_d)).astype(o_ref.dtype)


def normed_matmul(a, ssq, wt, layer, *, col=0, n=None, out_dtype, bm=1024, bn=1024, name):
    m, kd = a.shape
    n = wt.shape[1] if n is None else n
    bm, bn = _tile(m, bm, SUBLANES), _tile(math.gcd(n, col) if col else n, bn)
    first = col // bn
    return pl.pallas_call(
        functools.partial(_proj_kernel, inv_d=1.0 / kd),
        grid=(m // bm, n // bn),
        in_specs=[pl.BlockSpec((bm, kd), lambda i, j: (i, 0)),
                  pl.BlockSpec((bm, LANES), lambda i, j: (i, 0)),
                  pl.BlockSpec((None, bn, kd), lambda i, j: (layer, first + j, 0))],
        out_specs=pl.BlockSpec((bm, bn), lambda i, j: (i, j)),
        out_shape=jax.ShapeDtypeStruct((m, n), out_dtype),
        compiler_params=_params("parallel", "parallel"),
        name=name,
    )(a, ssq, wt)


def _residual_kernel(a_ref, w_ref, r_ref, *refs, nk, scale, emit_norm):
    if emit_norm:
        g_ref, o_ref, u_ref, ssq_ref, *scratch = refs
    else:
        o_ref, *scratch = refs

    def finish(acc):
        h = r_ref[...] + scale * acc
        o_ref[...] = h
        if emit_norm:
            u_ref[...] = (h * g_ref[...]).astype(u_ref.dtype)
            row_ssq = _lane_partial_ssq(h)
            j = pl.program_id(1)

            @pl.when(j == 0)
            def _():
                ssq_ref[...] = row_ssq

            @pl.when(j > 0)
            def _():
                ssq_ref[...] += row_ssq

    part = _dot(a_ref[...], w_ref[...])
    if nk == 1:
        finish(part)
        return
    acc_ref, = scratch
    k = pl.program_id(2)

    @pl.when(k == 0)
    def _():
        acc_ref[...] = part

    @pl.when(jnp.logical_and(k > 0, k < nk - 1))
    def _():
        acc_ref[...] += part

    @pl.when(k == nk - 1)
    def _():
        finish(acc_ref[...] + part)


def residual_matmul(a, w, layer, res, *, scale, gain=None, bm=1024, bn=1024, bk=None, name):
    m, kd = a.shape
    n = w.shape[2]
    bm, bn = _tile(m, bm, SUBLANES), _tile(n, bn)
    bk = kd if bk is None else _tile(kd, bk)
    nk = kd // bk
    tile = pl.BlockSpec((bm, bn), lambda i, j, k: (i, j))
    in_specs = [pl.BlockSpec((bm, bk), lambda i, j, k: (i, k)),
                pl.BlockSpec((None, bk, bn), lambda i, j, k: (layer, k, j)),
                tile]
    args = [a, w, res]
    out_specs, out_shape = tile, jax.ShapeDtypeStruct((m, n), F32)
    if gain is not None:
        in_specs.append(pl.BlockSpec((1, bn), lambda i, j, k: (0, j)))
        args.append(gain.reshape(1, n))
        out_specs = [tile, tile, pl.BlockSpec((bm, LANES), lambda i, j, k: (i, 0))]
        out_shape = [out_shape, jax.ShapeDtypeStruct((m, n), BF16), jax.ShapeDtypeStruct((m, LANES), F32)]
    return pl.pallas_call(
        functools.partial(_residual_kernel, nk=nk, scale=scale, emit_norm=gain is not None),
        grid=(m // bm, n // bn, nk),
        in_specs=in_specs,
        out_specs=out_specs,
        out_shape=out_shape,
        scratch_shapes=[pltpu.VMEM((bm, bn), F32)] if nk > 1 else [],
        compiler_params=_params("parallel", "arbitrary", "arbitrary"),
        name=name,
    )(*args)


def _swiglu_kernel(u_ref, ssq_ref, w1_ref, w3_ref, o_ref, *, inv_d):
    u = u_ref[...]
    rstd = _row_rstd(ssq_ref, inv_d)
    a = _dot(u, w1_ref[...]) * rstd
    b = _dot(u, w3_ref[...]) * rstd
    o_ref[...] = (a * jax.nn.sigmoid(a) * b).astype(o_ref.dtype)


def swiglu_up(u, ssq, w13, layer, *, bm=1024, bn=512):
    m, d = u.shape
    f = w13.shape[2] // 2
    bm, bn = _tile(m, bm, SUBLANES), _tile(f, bn)
    nf = f // bn
    return pl.pallas_call(
        functools.partial(_swiglu_kernel, inv_d=1.0 / d),
        grid=(m // bm, nf),
        in_specs=[pl.BlockSpec((bm, d), lambda i, j: (i, 0)),
                  pl.BlockSpec((bm, LANES), lambda i, j: (i, 0)),
                  pl.BlockSpec((None, d, bn), lambda i, j: (layer, 0, j)),
                  pl.BlockSpec((None, d, bn), lambda i, j: (layer, 0, j + nf))],
        out_specs=pl.BlockSpec((bm, bn), lambda i, j: (i, j)),
        out_shape=jax.ShapeDtypeStruct((m, f), BF16),
        compiler_params=_params("parallel", "parallel"),
        name="swiglu_up",
    )(u, ssq, w13, w13)


def _merge_kernel(u_ref, ssq_ref, ro_ref, fo_ref, mo_ref, wg_ref, bg_ref, wr_ref, wf_ref, wm_ref, o_ref, *,
                  inv_d):
    u = u_ref[...]
    rstd = _row_rstd(ssq_ref, inv_d)
    total = None
    for i, (x_ref, w_ref) in enumerate(((ro_ref, wr_ref), (fo_ref, wf_ref), (mo_ref, wm_ref))):
        gate = jax.nn.sigmoid(_dot(u, wg_ref[i]) * rstd + bg_ref[i:i + 1, :])
        term = gate * _dot(x_ref[...], w_ref[...])
        total = term if total is None else total + term
    o_ref[...] = total.astype(o_ref.dtype)


def gated_merge(u, ssq, ro, fo, mo, w_gate, b_gate, w_up_ret, w_up_fox, w_up_mla, layer, *, bm=512, bn=256):
    m, d = u.shape
    n = w_gate.shape[-1]
    bm, bn = _tile(m, bm, SUBLANES), _tile(n, bn)
    row = lambda width: pl.BlockSpec((bm, width), lambda i, j: (i, 0))
    col = lambda depth: pl.BlockSpec((None, depth, bn), lambda i, j: (layer, 0, j))
    return pl.pallas_call(
        functools.partial(_merge_kernel, inv_d=1.0 / d),
        grid=(m // bm, n // bn),
        in_specs=[row(d), row(LANES), row(ro.shape[1]), row(fo.shape[1]), row(mo.shape[1]),
                  pl.BlockSpec((None, 3, d, bn), lambda i, j: (layer, 0, 0, j)),
                  pl.BlockSpec((None, 3, bn), lambda i, j: (layer, 0, j)),
                  col(ro.shape[1]), col(fo.shape[1]), col(mo.shape[1])],
        out_specs=pl.BlockSpec((bm, bn), lambda i, j: (i, j)),
        out_shape=jax.ShapeDtypeStruct((m, n), BF16),
        compiler_params=_params("parallel", "parallel"),
        name="gated_merge",
    )(u, ssq, ro, fo, mo, w_gate, b_gate, w_up_ret, w_up_fox, w_up_mla)


def _rope_table_kernel(pos_ref, inv_ref, coef_ref, rc_ref, rs_ref, mc_ref, ma_ref, mb_ref):
    p = pos_ref[...].astype(F32)
    ang_r = p * inv_ref[0:1, :]
    ang_m = p * inv_ref[1:2, :]
    rc_ref[...] = jnp.cos(ang_r)
    rs_ref[...] = jnp.sin(ang_r) * coef_ref[0:1, :]
    sin_m = jnp.sin(ang_m)
    mc_ref[...] = jnp.cos(ang_m) * coef_ref[1:2, :]
    ma_ref[...] = sin_m * coef_ref[2:3, :]
    mb_ref[...] = sin_m * coef_ref[3:4, :]


def rope_tables(positions):
    t = positions.size
    h_r, h_m = RET_DK // 2, MLA_ROPE // 2
    inv_r = ROPE_BASE ** (-jnp.arange(h_r, dtype=F32) / h_r)
    inv_m = ROPE_BASE ** (-jnp.arange(h_m, dtype=F32) / h_m)
    zeros_m = jnp.zeros((LANES - 2 * h_m,), F32)
    ones_m = jnp.ones((h_m,), F32)
    inv = jnp.stack([jnp.concatenate([inv_r, inv_r]),
                     jnp.concatenate([inv_m, inv_m, zeros_m])])
    coef = jnp.stack([
        jnp.concatenate([-jnp.ones((h_r,), F32), jnp.ones((h_r,), F32)]),
        jnp.concatenate([ones_m, ones_m, zeros_m]),
        jnp.concatenate([-ones_m, 0 * ones_m, zeros_m]),
        jnp.concatenate([0 * ones_m, ones_m, zeros_m]),
    ])
    bm = _tile(t, 1024, SUBLANES)
    tab = jax.ShapeDtypeStruct((t, LANES), F32)
    return pl.pallas_call(
        _rope_table_kernel,
        grid=(t // bm,),
        in_specs=[pl.BlockSpec((bm, 1), lambda i: (i, 0)),
                  pl.BlockSpec((2, LANES), lambda i: (0, 0)),
                  pl.BlockSpec((4, LANES), lambda i: (0, 0))],
        out_specs=[pl.BlockSpec((bm, LANES), lambda i: (i, 0))] * 5,
        out_shape=[tab] * 5,
        compiler_params=_params("parallel"),
        name="rope_tables",
    )(positions.reshape(t, 1), inv, coef)


def _ret_log_gamma(h):
    return math.log1p(-2.0 ** (-5.0 - h))


def _retention_kernel(q_ref, k_ref, v_ref, g_ref, rc_ref, rs_ref, norm_ref, o_ref, state_ref, decay_ref,
                      *, blk):
    first = jnp.logical_and(pl.program_id(0) == 0, pl.program_id(1) == 0)

    @pl.when(first)
    def _():
        row = lax.broadcasted_iota(jnp.int32, (blk, blk), 0)
        col = lax.broadcasted_iota(jnp.int32, (blk, blk), 1)
        dist = jnp.abs(row - col).astype(F32)
        visible = (col // CHUNK) <= (row // CHUNK)
        for h in range(RET_HEADS):
            decay_ref[h] = jnp.where(visible, jnp.exp(_ret_log_gamma(h) * dist), 0.0)

    @pl.when(pl.program_id(1) == 0)
    def _():
        state_ref[...] = jnp.zeros_like(state_ref)

    rc = rc_ref[0]
    rs = rs_ref[0]
    idx = lax.broadcasted_iota(jnp.int32, (blk, 1), 0).astype(F32)
    for h in range(RET_HEADS):
        lg = _ret_log_gamma(h)
        qk = slice(h * RET_DK, (h + 1) * RET_DK)
        vv = slice(h * RET_DV, (h + 1) * RET_DV)
        q = q_ref[0, :, qk]
        k = k_ref[0, :, qk]
        q = (q * rc + pltpu.roll(q, RET_DK // 2, 1) * rs) * (RET_DK ** -0.5)
        k = k * rc + pltpu.roll(k, RET_DK // 2, 1) * rs
        v = v_ref[0, :, vv]
        state = state_ref[h]
        scores = _dot_nt(q.astype(BF16), k.astype(BF16)) * decay_ref[h]
        q_in = (q * jnp.exp(lg * (idx + 1.0))).astype(BF16)
        out = _dot(scores.astype(BF16), v) + _dot(q_in, state.astype(BF16))
        k_out = (k * jnp.exp(lg * (blk - 1.0 - idx))).astype(BF16)
        state_ref[h] = state * math.exp(lg * blk) + _dot_tn(k_out, v)
        out = out * lax.rsqrt(jnp.mean(out * out, axis=-1, keepdims=True) + EPS)
        gate = g_ref[0, :, vv].astype(F32)
        out = out * norm_ref[:, vv] * (gate * jax.nn.sigmoid(gate))
        o_ref[0, :, vv] = out.astype(o_ref.dtype)


def retention(za, zb, rc, rs, ret_norm, *, blk=256):
    b, s, _ = za.shape
    blk = _tile(s, blk, CHUNK)
    qw, vw = RET_HEADS * RET_DK, RET_HEADS * RET_DV
    return pl.pallas_call(
        functools.partial(_retention_kernel, blk=blk),
        grid=(b, s // blk),
        in_specs=[pl.BlockSpec((1, blk, qw), lambda i, j: (i, j, A_RQ // qw)),
                  pl.BlockSpec((1, blk, qw), lambda i, j: (i, j, A_RK // qw)),
                  pl.BlockSpec((1, blk, vw), lambda i, j: (i, j, B_RV // vw)),
                  pl.BlockSpec((1, blk, vw), lambda i, j: (i, j, B_RG // vw)),
                  pl.BlockSpec((1, blk, LANES), lambda i, j: (i, j, 0)),
                  pl.BlockSpec((1, blk, LANES), lambda i, j: (i, j, 0)),
                  pl.BlockSpec((1, vw), lambda i, j: (0, 0))],
        out_specs=pl.BlockSpec((1, blk, vw), lambda i, j: (i, j, 0)),
        out_shape=jax.ShapeDtypeStruct((b, s, vw), BF16),
        scratch_shapes=[pltpu.VMEM((RET_HEADS, RET_DK, RET_DV), F32),
                        pltpu.VMEM((RET_HEADS, blk, blk), F32)],
        compiler_params=_params("arbitrary", "arbitrary"),
        name="retention",
    )(za, za, zb, zb, rc, rs, ret_norm.reshape(1, vw))


def _split3(x):
    hi = x.astype(BF16).astype(F32)
    rest = x - hi
    mid = rest.astype(BF16).astype(F32)
    lo = (rest - mid).astype(BF16).astype(F32)
    return hi, mid, lo


def _forget_cumsum_kernel(z_ref, bias_ref, qa_ref, ka_ref, carry_ref, *, rows):
    @pl.when(pl.program_id(1) == 0)
    def _():
        carry_ref[...] = jnp.zeros_like(carry_ref)

    lane = lax.broadcasted_iota(jnp.int32, (rows, LANES), 1)
    is_gate = jnp.logical_and(lane >= FF_LANE, lane < FF_LANE + FOX_HEADS)
    x = z_ref[0] + bias_ref[...]
    log_f = jnp.minimum(x, 0.0) - jnp.log(1.0 + jnp.exp(-jnp.abs(x)))
    log_f = jnp.where(is_gate, log_f * LOG2_E, 0.0)
    tri = (lax.broadcasted_iota(jnp.int32, (rows, rows), 0)
           >= lax.broadcasted_iota(jnp.int32, (rows, rows), 1)).astype(BF16)
    cum = carry_ref[...]
    for piece in _split3(log_f):
        cum = cum + _dot(tri, piece.astype(BF16))
    carry_ref[...] = cum[rows - 1:rows, :]
    one = jnp.where(lane < 2 * N_SPLIT, 1.0, 0.0)
    for h in range(FOX_HEADS):
        col = jnp.sum(jnp.where(lane == FF_LANE + h, cum, 0.0), axis=1, keepdims=True)
        pieces = _split3(jnp.broadcast_to(col, (rows, LANES)))
        qa = jnp.where(lane < N_SPLIT, 0.0, one)
        ka = jnp.where(lane < N_SPLIT, one, 0.0)
        for i, piece in enumerate(pieces):
            qa = jnp.where(lane == i, piece, qa)
            ka = jnp.where(lane == N_SPLIT + i, -piece, ka)
        qa_ref[0, :, h * LANES:(h + 1) * LANES] = qa.astype(qa_ref.dtype)
        ka_ref[0, :, h * LANES:(h + 1) * LANES] = ka.astype(ka_ref.dtype)


def forget_bias_lanes(za, b_forget, *, rows=256):
    b, s, _ = za.shape
    rows = _tile(s, rows)
    bias = jnp.zeros((1, LANES), F32).at[0, FF_LANE:FF_LANE + FOX_HEADS].set(b_forget)
    aux = jax.ShapeDtypeStruct((b, s, FOX_HEADS * LANES), BF16)
    return pl.pallas_call(
        functools.partial(_forget_cumsum_kernel, rows=rows),
        grid=(b, s // rows),
        in_specs=[pl.BlockSpec((1, rows, LANES), lambda i, j: (i, j, A_KR // LANES)),
                  pl.BlockSpec((1, LANES), lambda i, j: (0, 0))],
        out_specs=[pl.BlockSpec((1, rows, FOX_HEADS * LANES), lambda i, j: (i, j, 0))] * 2,
        out_shape=[aux, aux],
        scratch_shapes=[pltpu.VMEM((1, LANES), F32)],
        compiler_params=_params("arbitrary", "arbitrary"),
        name="forget_bias_lanes",
    )(za, bias)


def _attn_kernel(qi_ref, ki_ref, *refs, has_aux, chunked_mask, hp, dk, dv, t):
    if has_aux:
        q_ref, qa_ref, k_ref, ka_ref, v_ref, o_ref, m_ref, l_ref, acc_ref = refs
    else:
        q_ref, k_ref, v_ref, o_ref, m_ref, l_ref, acc_ref = refs
    pair = pl.program_id(2)
    qi, ki = qi_ref[pair], ki_ref[pair]

    @pl.when(ki == 0)
    def _():
        m_ref[...] = jnp.full_like(m_ref, NEG_BIG)
        l_ref[...] = jnp.zeros_like(l_ref)
        acc_ref[...] = jnp.zeros_like(acc_ref)

    def step(diagonal):
        if diagonal:
            row = lax.broadcasted_iota(jnp.int32, (t, t), 0)
            col = lax.broadcasted_iota(jnp.int32, (t, t), 1)
            keep = (col // CHUNK) <= (row // CHUNK) if chunked_mask else col <= row
        for h in range(hp):
            q = q_ref[0, :, h * dk:(h + 1) * dk]
            k = k_ref[0, :, h * dk:(h + 1) * dk]
            if has_aux:
                q = jnp.concatenate([q, qa_ref[0, :, h * LANES:(h + 1) * LANES]], axis=1)
                k = jnp.concatenate([k, ka_ref[0, :, h * LANES:(h + 1) * LANES]], axis=1)
            s = _dot_nt(q, k)
            if diagonal:
                s = jnp.where(keep, s, NEG_BIG)
            m_old = m_ref[h]
            m_new = jnp.maximum(m_old, jnp.max(s, axis=1, keepdims=True))
            alpha = jnp.exp2(m_old - m_new)
            p = jnp.exp2(s - jnp.concatenate([m_new] * (t // LANES), axis=1))
            l_ref[h] = alpha * l_ref[h] + jnp.sum(p, axis=1, keepdims=True)
            acc_ref[h] = alpha * acc_ref[h] + _dot(p.astype(BF16), v_ref[0, :, h * dv:(h + 1) * dv])
            m_ref[h] = m_new

    @pl.when(ki < qi)
    def _():
        step(False)

    @pl.when(ki == qi)
    def _():
        step(True)
        for h in range(hp):
            o_ref[0, :, h * dv:(h + 1) * dv] = (acc_ref[h] / l_ref[h]).astype(o_ref.dtype)


def attention(q_arr, q_col, k_arr, k_col, v_arr, v_col, *, heads, dk, dv, aux=None, chunked_mask,
              tile=512, hp=ATTN_HEADS_PER_STEP):
    assert dv == LANES and heads % hp == 0
    b, s, _ = q_arr.shape
    t = _tile(s, tile)
    n = s // t
    pairs = [(i, j) for i in range(n) for j in range(i + 1)]
    qi_tab = jnp.asarray([p[0] for p in pairs], jnp.int32)
    ki_tab = jnp.asarray([p[1] for p in pairs], jnp.int32)

    def spec(width, col, table):
        assert col % (hp * width) == 0
        first = col // (hp * width)
        if table == "q":
            return pl.BlockSpec((1, t, hp * width), lambda i, g, p, qt, kt: (i, qt[p], first + g))
        return pl.BlockSpec((1, t, hp * width), lambda i, g, p, qt, kt: (i, kt[p], first + g))

    if aux is None:
        in_specs = [spec(dk, q_col, "q"), spec(dk, k_col, "k"), spec(dv, v_col, "k")]
        args = [q_arr, k_arr, v_arr]
    else:
        in_specs = [spec(dk, q_col, "q"), spec(LANES, 0, "q"), spec(dk, k_col, "k"), spec(LANES, 0, "k"),
                    spec(dv, v_col, "k")]
        args = [q_arr, aux[0], k_arr, aux[1], v_arr]
    stat = pltpu.VMEM((hp, t, LANES), F32)
    return pl.pallas_call(
        functools.partial(_attn_kernel, has_aux=aux is not None, chunked_mask=chunked_mask, hp=hp, dk=dk,
                          dv=dv, t=t),
        grid_spec=pltpu.PrefetchScalarGridSpec(
            num_scalar_prefetch=2,
            grid=(b, heads // hp, len(pairs)),
            in_specs=in_specs,
            out_specs=spec(dv, 0, "q"),
            scratch_shapes=[stat, stat, pltpu.VMEM((hp, t, dv), F32)]),
        out_shape=jax.ShapeDtypeStruct((b, s, heads * dv), BF16),
        compiler_params=_params("parallel", "parallel", "arbitrary"),
        name="fox_attention" if aux is not None else "mla_attention",
    )(qi_tab, ki_tab, *args)


def _mla_rope(x, mc, ma, mb):
    return x * mc + pltpu.roll(x, LANES - MLA_ROPE // 2, 1) * ma + pltpu.roll(x, MLA_ROPE // 2, 1) * mb


def _mla_q_kernel(c_ref, g_ref, w_ref, mc_ref, ma_ref, mb_ref, o_ref):
    x = c_ref[...]
    xn = (x * lax.rsqrt(jnp.mean(x * x, axis=-1, keepdims=True) + EPS) * g_ref[...]).astype(BF16)
    q = _dot(xn, w_ref[...])
    scale = (MLA_NOPE + MLA_ROPE) ** -0.5 * LOG2_E
    mc, ma, mb = mc_ref[...], ma_ref[...], mb_ref[...]
    for h in range(MLA_HEADS):
        lo = h * MLA_QK_PAD
        o_ref[:, lo:lo + MLA_NOPE] = (q[:, lo:lo + MLA_NOPE] * scale).astype(o_ref.dtype)
        pe = _mla_rope(q[:, lo + MLA_NOPE:lo + MLA_QK_PAD], mc, ma, mb)
        o_ref[:, lo + MLA_NOPE:lo + MLA_QK_PAD] = (pe * scale).astype(o_ref.dtype)


def mla_queries(za2, q_norm, w_uq_pad, layer, mc, ma, mb, *, bm=512):
    t = za2.shape[0]
    bm = _tile(t, bm, SUBLANES)
    tab = pl.BlockSpec((bm, LANES), lambda i: (i, 0))
    return pl.pallas_call(
        _mla_q_kernel,
        grid=(t // bm,),
        in_specs=[pl.BlockSpec((bm, MLA_Q_LORA), lambda i: (i, A_CQ // MLA_Q_LORA)),
                  pl.BlockSpec((1, MLA_Q_LORA), lambda i: (0, 0)),
                  pl.BlockSpec((None,) + w_uq_pad.shape[1:], lambda i: (layer, 0, 0)),
                  tab, tab, tab],
        out_specs=pl.BlockSpec((bm, MLA_HEADS * MLA_QK_PAD), lambda i: (i, 0)),
        out_shape=jax.ShapeDtypeStruct((t, MLA_HEADS * MLA_QK_PAD), BF16),
        compiler_params=_params("parallel"),
        name="mla_queries",
    )(za2, q_norm.reshape(1, -1), w_uq_pad, mc, ma, mb)


def _mla_kv_kernel(c_ref, kr_ref, g_ref, wk_ref, wv_ref, mc_ref, ma_ref, mb_ref, k_ref, v_ref):
    x = c_ref[...]
    xn = (x * lax.rsqrt(jnp.mean(x * x, axis=-1, keepdims=True) + EPS) * g_ref[...]).astype(BF16)
    k_nope = _dot(xn, wk_ref[...])
    v_ref[...] = _dot(xn, wv_ref[...]).astype(v_ref.dtype)
    k_pe = _mla_rope(kr_ref[...], mc_ref[...], ma_ref[...], mb_ref[...]).astype(k_ref.dtype)
    for h in range(MLA_HEADS):
        lo = h * MLA_QK_PAD
        k_ref[:, lo:lo + MLA_NOPE] = k_nope[:, h * MLA_NOPE:(h + 1) * MLA_NOPE].astype(k_ref.dtype)
        k_ref[:, lo + MLA_NOPE:lo + MLA_QK_PAD] = k_pe


def mla_keys_values(za2, kv_norm, wk, wv, layer, mc, ma, mb, *, bm=512):
    t = za2.shape[0]
    bm = _tile(t, bm, SUBLANES)
    tab = pl.BlockSpec((bm, LANES), lambda i: (i, 0))
    w_spec = pl.BlockSpec((None,) + wk.shape[1:], lambda i: (layer, 0, 0))
    return pl.pallas_call(
        _mla_kv_kernel,
        grid=(t // bm,),
        in_specs=[pl.BlockSpec((bm, MLA_KV_LORA), lambda i: (i, A_CKV // MLA_KV_LORA)),
                  pl.BlockSpec((bm, LANES), lambda i: (i, A_KR // LANES)),
                  pl.BlockSpec((1, MLA_KV_LORA), lambda i: (0, 0)),
                  w_spec, w_spec, tab, tab, tab],
        out_specs=[pl.BlockSpec((bm, MLA_HEADS * MLA_QK_PAD), lambda i: (i, 0)),
                   pl.BlockSpec((bm, MLA_HEADS * MLA_DV), lambda i: (i, 0))],
        out_shape=[jax.ShapeDtypeStruct((t, MLA_HEADS * MLA_QK_PAD), BF16),
                   jax.ShapeDtypeStruct((t, MLA_HEADS * MLA_DV), BF16)],
        compiler_params=_params("parallel"),
        name="mla_keys_values",
    )(za2, za2, kv_norm.reshape(1, -1), wk, wv, mc, ma, mb)


def _prepare_weights(p):
    w_in_t = jnp.swapaxes(p["w_in"], 1, 2)
    depth, width, d = w_in_t.shape
    rq_rk_end = 2 * RET_HEADS * RET_DK
    bf_end = rq_rk_end + B_WIDTH
    ff_end = bf_end + FOX_HEADS
    fq0 = rq_rk_end + B_FQ
    row_scale = jnp.ones((width, 1), F32).at[fq0:fq0 + FOX_HEADS * FOX_DH].set(FOX_DH ** -0.5 * LOG2_E)
    w_in_b = (w_in_t * row_scale).astype(BF16)
    used = rq_rk_end + (width - ff_end) + FOX_HEADS
    w_in_a = jnp.concatenate([w_in_b[:, :rq_rk_end], w_in_b[:, ff_end:], w_in_b[:, bf_end:ff_end],
                              jnp.zeros((depth, A_WIDTH - used, d), BF16)], axis=1)
    w_uq = p["w_uq"].reshape(depth, MLA_Q_LORA, MLA_HEADS, MLA_NOPE + MLA_ROPE)
    w_uq = jnp.pad(w_uq, ((0, 0), (0, 0), (0, 0), (0, MLA_QK_PAD - MLA_NOPE - MLA_ROPE)))
    w_ukv = p["w_ukv"].reshape(depth, MLA_KV_LORA, MLA_HEADS, MLA_NOPE + MLA_DV)
    out = {k: p[k].astype(BF16) for k in ("ffn1_w13", "ffn1_w2", "ffn2_w13", "ffn2_w2", "w_gate", "w_out",
                                          "w_up_ret", "w_up_fox", "w_up_mla")}
    out.update(
        w_in_a=w_in_a, w_in_b=w_in_b, w_in_b_col=rq_rk_end,
        w_uq=w_uq.reshape(depth, MLA_Q_LORA, MLA_HEADS * MLA_QK_PAD).astype(BF16),
        w_uk=w_ukv[..., :MLA_NOPE].reshape(depth, MLA_KV_LORA, MLA_HEADS * MLA_NOPE).astype(BF16),
        w_uv=w_ukv[..., MLA_NOPE:].reshape(depth, MLA_KV_LORA, MLA_HEADS * MLA_DV).astype(BF16))
    return out


def _ffn(h, u, ssq, w13, w2, layer, next_gain):
    g = swiglu_up(u, ssq, w13, layer)
    return residual_matmul(g, w2, layer, h, scale=0.5, gain=next_gain, bm=512, bn=512, name="ffn_down")


def _mixer(h, u, ssq, batch, tables, p, w, layer, next_gain):
    rc, rs, mc, ma, mb = tables
    t, d = h.shape
    s = t // batch
    za = normed_matmul(u, ssq, w["w_in_a"], layer, out_dtype=F32, name="in_proj_f32")
    zb = normed_matmul(u, ssq, w["w_in_b"], layer, col=w["w_in_b_col"], n=B_WIDTH, out_dtype=BF16,
                       name="in_proj_bf16")
    za3 = za.reshape(batch, s, A_WIDTH)
    zb3 = zb.reshape(batch, s, B_WIDTH)
    tab3 = lambda x: x.reshape(batch, s, LANES)

    ro = retention(za3, zb3, tab3(rc), tab3(rs), p["ret_norm"][layer])
    fo = attention(zb3, B_FQ, zb3, B_FK, zb3, B_FV, heads=FOX_HEADS, dk=FOX_DH, dv=FOX_DH,
                   aux=forget_bias_lanes(za3, p["b_forget"][layer]), chunked_mask=False)
    qm = mla_queries(za, p["mla_q_norm"][layer], w["w_uq"], layer, mc, ma, mb)
    km, vm = mla_keys_values(za, p["mla_kv_norm"][layer], w["w_uk"], w["w_uv"], layer, mc, ma, mb)
    mo = attention(qm.reshape(batch, s, -1), 0, km.reshape(batch, s, -1), 0, vm.reshape(batch, s, -1), 0,
                   heads=MLA_HEADS, dk=MLA_QK_PAD, dv=MLA_DV, chunked_mask=True)

    merged = gated_merge(u, ssq, ro.reshape(t, -1), fo.reshape(t, -1), mo.reshape(t, -1), w["w_gate"],
                         p["b_gate"], w["w_up_ret"], w["w_up_fox"], w["w_up_mla"], layer)
    return residual_matmul(merged, w["w_out"], layer, h, scale=1.0, gain=next_gain, bn=512, name="out_proj")


def kernel(x, positions, ffn1_norm, ffn1_w13, ffn1_w2, mix_norm, w_in, b_forget, ret_norm, mla_q_norm,
           mla_kv_norm, w_uq, w_ukv, w_up_ret, w_up_fox, w_up_mla, w_gate, b_gate, w_out, ffn2_norm,
           ffn2_w13, ffn2_w2, final_norm):
    p = dict(mix_norm=mix_norm, w_in=w_in, b_forget=b_forget, ret_norm=ret_norm, mla_q_norm=mla_q_norm,
             mla_kv_norm=mla_kv_norm, w_uq=w_uq, w_ukv=w_ukv, w_up_ret=w_up_ret, w_up_fox=w_up_fox,
             w_up_mla=w_up_mla, w_gate=w_gate, b_gate=b_gate, w_out=w_out, ffn1_w13=ffn1_w13,
             ffn1_w2=ffn1_w2, ffn2_w13=ffn2_w13, ffn2_w2=ffn2_w2)
    batch, s, d = x.shape
    h = x.reshape(batch * s, d)
    w = _prepare_weights(p)
    tables = rope_tables(positions)
    depth = ffn1_norm.shape[0]
    u, ssq = prenorm(h, ffn1_norm[0])
    for layer in range(depth):
        h, u, ssq = _ffn(h, u, ssq, w["ffn1_w13"], w["ffn1_w2"], layer, mix_norm[layer])
        h, u, ssq = _mixer(h, u, ssq, batch, tables, p, w, layer, ffn2_norm[layer])
        if layer + 1 < depth:
            h, u, ssq = _ffn(h, u, ssq, w["ffn2_w13"], w["ffn2_w2"], layer, ffn1_norm[layer + 1])
        else:
            h = _ffn(h, u, ssq, w["ffn2_w13"], w["ffn2_w2"], layer, None)
    return rmsnorm(h, final_norm, x.dtype).reshape(batch, s, d)
```

```python
import functools
import math

import jax
import jax.numpy as jnp
from jax import lax
from jax.experimental import pallas as pl
from jax.experimental.pallas import tpu as pltpu

F32 = jnp.float32
BF16 = jnp.bfloat16

CHUNK = 64
RET_HEADS, RET_DK, RET_DV = 8, 128, 256
FOX_HEADS, FOX_DH = 8, 128
MLA_HEADS, MLA_Q_LORA, MLA_KV_LORA = 8, 1024, 512
MLA_NOPE, MLA_ROPE, MLA_DV = 128, 64, 128
MLA_QK_PAD = 256
ROPE_BASE = 10000.0
EPS = 1e-6
NEG_BIG = -1e30
LOG2_E = math.log2(math.e)
N_SPLIT = 3
ATTN_HEADS_PER_STEP = 8

LANES = 128
SUBLANES = 8
BF16_ROWS = 2 * SUBLANES
VMEM_LIMIT_BYTES = 62 * 1024 * 1024

A_RQ, A_RK, A_CQ, A_CKV, A_KR = 0, 1024, 2048, 3072, 3584
A_FF = A_KR + MLA_ROPE
A_WIDTH = 4096
FF_LANE = A_FF - A_KR
B_RV, B_RG, B_FQ, B_FK, B_FV = 0, 2048, 4096, 5120, 6144
B_WIDTH = 7168


def _tile(dim, pref, quantum=LANES):
    if dim <= pref:
        return dim
    t = (pref // quantum) * quantum
    while t > quantum and dim % t:
        t -= quantum
    assert dim % t == 0, (dim, pref)
    return t


def _params(*sem):
    return pltpu.CompilerParams(dimension_semantics=sem, vmem_limit_bytes=VMEM_LIMIT_BYTES)


def _dot(a, b):
    return jnp.dot(a, b, preferred_element_type=F32)


def _dot_nt(a, b):
    return lax.dot_general(a, b, (((1,), (1,)), ((), ())), preferred_element_type=F32)


def _dot_tn(a, b):
    return lax.dot_general(a, b, (((0,), (0,)), ((), ())), preferred_element_type=F32)


def _cast_plan(casts, grid):
    steps = math.prod(grid)

    def linear(ids):
        idx = ids[0]
        for extent, i in zip(grid[1:], ids[1:]):
            idx = idx * extent + i
        return idx

    in_specs, out_specs, out_shapes, operands = [], [], [], []
    for stack, layer in casts:
        _, rows, cols = stack.shape
        blocks = max(n for n in range(1, steps + 1)
                     if steps % n == 0 and rows % n == 0 and (rows // n) % BF16_ROWS == 0)
        rb, rep = rows // blocks, steps // blocks
        in_specs.append(pl.BlockSpec((None, rb, cols),
                                     lambda *ids, layer=layer, rep=rep: (layer, linear(ids) // rep, 0)))
        out_specs.append(pl.BlockSpec((rb, cols), lambda *ids, rep=rep: (linear(ids) // rep, 0)))
        out_shapes.append(jax.ShapeDtypeStruct((rows, cols), BF16))
        operands.append(stack)
    return in_specs, out_specs, out_shapes, operands


def _hosting(body, n_in, n_out, n_cast):
    if not n_cast:
        return body

    def kernel(*refs):
        ins, rest = refs[:n_in], refs[n_in:]
        srcs, rest = rest[:n_cast], rest[n_cast:]
        outs, rest = rest[:n_out], rest[n_out:]
        dsts, scratch = rest[:n_cast], rest[n_cast:]
        body(*ins, *outs, *scratch)
        for src, dst in zip(srcs, dsts):
            dst[...] = src[...].astype(dst.dtype)

    return kernel


def _rmsnorm_kernel(x_ref, g_ref, o_ref):
    x = x_ref[...]
    ms = jnp.mean(x * x, axis=-1, keepdims=True)
    o_ref[...] = (x * lax.rsqrt(ms + EPS) * g_ref[...]).astype(o_ref.dtype)


def rmsnorm(x, g, out_dtype):
    m, d = x.shape
    bm = _tile(m, 512, SUBLANES)
    return pl.pallas_call(
        _rmsnorm_kernel,
        grid=(m // bm,),
        in_specs=[pl.BlockSpec((bm, d), lambda i: (i, 0)),
                  pl.BlockSpec((1, d), lambda i: (0, 0))],
        out_specs=pl.BlockSpec((bm, d), lambda i: (i, 0)),
        out_shape=jax.ShapeDtypeStruct((m, d), out_dtype),
        compiler_params=_params("parallel"),
        name="rmsnorm",
    )(x, g.reshape(1, d))


def _lane_partial_ssq(x):
    sq = x * x
    part = sq[:, :LANES]
    for c in range(1, x.shape[1] // LANES):
        part = part + sq[:, c * LANES:(c + 1) * LANES]
    return part


def _row_rstd(ssq_ref, inv_d):
    return lax.rsqrt(jnp.sum(ssq_ref[...], axis=1, keepdims=True) * inv_d + EPS)


def _prenorm_kernel(x_ref, g_ref, u_ref, ssq_ref):
    x = x_ref[...]
    u_ref[...] = (x * g_ref[...]).astype(u_ref.dtype)
    ssq_ref[...] = _lane_partial_ssq(x)


def prenorm(x, g):
    m, d = x.shape
    bm = _tile(m, 512, SUBLANES)
    return pl.pallas_call(
        _prenorm_kernel,
        grid=(m // bm,),
        in_specs=[pl.BlockSpec((bm, d), lambda i: (i, 0)),
                  pl.BlockSpec((1, d), lambda i: (0, 0))],
        out_specs=[pl.BlockSpec((bm, d), lambda i: (i, 0)),
                   pl.BlockSpec((bm, LANES), lambda i: (i, 0))],
        out_shape=[jax.ShapeDtypeStruct((m, d), BF16), jax.ShapeDtypeStruct((m, LANES), F32)],
        compiler_params=_params("parallel"),
        name="prenorm",
    )(x, g.reshape(1, d))


def _proj_kernel(a_ref, ssq_ref, wt_ref, o_ref, *, inv_d):
    acc = _dot_nt(a_ref[...], wt_ref[...])
    o_ref[...] = (acc * _row_rstd(ssq_ref, inv_d)).astype(o_ref.dtype)


def normed_matmul(a, ssq, wt, layer, *, col=0, n=None, out_dtype, casts=(), bm=1024, bn=1024, name):
    m, kd = a.shape
    n = wt.shape[1] if n is None else n
    bm, bn = _tile(m, bm, SUBLANES), _tile(math.gcd(n, col) if col else n, bn)
    first = col // bn
    grid = (m // bm, n // bn)
    c_in, c_out, c_shape, c_args = _cast_plan(casts, grid)
    out, *casted = pl.pallas_call(
        _hosting(functools.partial(_proj_kernel, inv_d=1.0 / kd), 3, 1, len(casts)),
        grid=grid,
        in_specs=[pl.BlockSpec((bm, kd), lambda i, j: (i, 0)),
                  pl.BlockSpec((bm, LANES), lambda i, j: (i, 0)),
                  pl.BlockSpec((None, bn, kd), lambda i, j: (layer, first + j, 0))] + c_in,
        out_specs=[pl.BlockSpec((bm, bn), lambda i, j: (i, j))] + c_out,
        out_shape=[jax.ShapeDtypeStruct((m, n), out_dtype)] + c_shape,
        compiler_params=_params("parallel", "parallel"),
        name=name,
    )(a, ssq, wt, *c_args)
    return out, casted


def _residual_kernel(a_ref, w_ref, r_ref, *refs, nk, scale, emit_norm):
    if emit_norm:
        g_ref, o_ref, u_ref, ssq_ref, *scratch = refs
    else:
        o_ref, *scratch = refs

    def finish(acc):
        h = r_ref[...] + scale * acc
        o_ref[...] = h
        if emit_norm:
            u_ref[...] = (h * g_ref[...]).astype(u_ref.dtype)
            row_ssq = _lane_partial_ssq(h)
            j = pl.program_id(1)

            @pl.when(j == 0)
            def _():
                ssq_ref[...] = row_ssq

            @pl.when(j > 0)
            def _():
                ssq_ref[...] += row_ssq

    part = _dot(a_ref[...], w_ref[...])
    if nk == 1:
        finish(part)
        return
    acc_ref, = scratch
    k = pl.program_id(2)

    @pl.when(k == 0)
    def _():
        acc_ref[...] = part

    @pl.when(jnp.logical_and(k > 0, k < nk - 1))
    def _():
        acc_ref[...] += part

    @pl.when(k == nk - 1)
    def _():
        finish(acc_ref[...] + part)


def residual_matmul(a, w, layer, res, *, scale, gain=None, casts=(), bm=1024, bn=1024, bk=None, name):
    m, kd = a.shape
    n = w.shape[2]
    bm, bn = _tile(m, bm, SUBLANES), _tile(n, bn)
    bk = kd if bk is None else _tile(kd, bk)
    nk = kd // bk
    tile = pl.BlockSpec((bm, bn), lambda i, j, k: (i, j))
    in_specs = [pl.BlockSpec((bm, bk), lambda i, j, k: (i, k)),
                pl.BlockSpec((None, bk, bn), lambda i, j, k: (layer, k, j)),
                tile]
    args = [a, w, res]
    out_specs, out_shape = [tile], [jax.ShapeDtypeStruct((m, n), F32)]
    if gain is not None:
        in_specs.append(pl.BlockSpec((1, bn), lambda i, j, k: (0, j)))
        args.append(gain.reshape(1, n))
        out_specs += [tile, pl.BlockSpec((bm, LANES), lambda i, j, k: (i, 0))]
        out_shape += [jax.ShapeDtypeStruct((m, n), BF16), jax.ShapeDtypeStruct((m, LANES), F32)]
    n_out = len(out_specs)
    grid = (m // bm, n // bn, nk)
    c_in, c_out, c_shape, c_args = _cast_plan(casts, grid)
    outs = pl.pallas_call(
        _hosting(functools.partial(_residual_kernel, nk=nk, scale=scale, emit_norm=gain is not None),
                 len(args), n_out, len(casts)),
        grid=grid,
        in_specs=in_specs + c_in,
        out_specs=out_specs + c_out,
        out_shape=out_shape + c_shape,
        scratch_shapes=[pltpu.VMEM((bm, bn), F32)] if nk > 1 else [],
        compiler_params=_params("parallel", "arbitrary", "arbitrary"),
        name=name,
    )(*args, *c_args)
    result = outs[0] if gain is None else tuple(outs[:n_out])
    return result, list(outs[n_out:])


def _swiglu_kernel(u_ref, ssq_ref, w1_ref, w3_ref, o_ref, *, inv_d):
    u = u_ref[...]
    rstd = _row_rstd(ssq_ref, inv_d)
    a = _dot(u, w1_ref[...]) * rstd
    b = _dot(u, w3_ref[...]) * rstd
    o_ref[...] = (a * jax.nn.sigmoid(a) * b).astype(o_ref.dtype)


def swiglu_up(u, ssq, w13, layer, *, casts=(), bm=1024, bn=512):
    m, d = u.shape
    f = w13.shape[2] // 2
    bm, bn = _tile(m, bm, SUBLANES), _tile(f, bn)
    nf = f // bn
    grid = (m // bm, nf)
    c_in, c_out, c_shape, c_args = _cast_plan(casts, grid)
    out, *casted = pl.pallas_call(
        _hosting(functools.partial(_swiglu_kernel, inv_d=1.0 / d), 4, 1, len(casts)),
        grid=grid,
        in_specs=[pl.BlockSpec((bm, d), lambda i, j: (i, 0)),
                  pl.BlockSpec((bm, LANES), lambda i, j: (i, 0)),
                  pl.BlockSpec((None, d, bn), lambda i, j: (layer, 0, j)),
                  pl.BlockSpec((None, d, bn), lambda i, j: (layer, 0, j + nf))] + c_in,
        out_specs=[pl.BlockSpec((bm, bn), lambda i, j: (i, j))] + c_out,
        out_shape=[jax.ShapeDtypeStruct((m, f), BF16)] + c_shape,
        compiler_params=_params("parallel", "parallel"),
        name="swiglu_up",
    )(u, ssq, w13, w13, *c_args)
    return out, casted


def _merge_kernel(u_ref, ssq_ref, ro_ref, fo_ref, mo_ref, wg_ref, bg_ref, wr_ref, wf_ref, wm_ref, o_ref, *,
                  inv_d):
    u = u_ref[...]
    rstd = _row_rstd(ssq_ref, inv_d)
    total = None
    for i, (x_ref, w_ref) in enumerate(((ro_ref, wr_ref), (fo_ref, wf_ref), (mo_ref, wm_ref))):
        gate = jax.nn.sigmoid(_dot(u, wg_ref[i]) * rstd + bg_ref[i:i + 1, :])
        term = gate * _dot(x_ref[...], w_ref[...])
        total = term if total is None else total + term
    o_ref[...] = total.astype(o_ref.dtype)


def gated_merge(u, ssq, ro, fo, mo, w_gate, b_gate, layer, w_up_ret, w_up_fox, w_up_mla, *, casts=(),
                bm=512, bn=512):
    m, d = u.shape
    n = w_gate.shape[-1]
    bm, bn = _tile(m, bm, SUBLANES), _tile(n, bn)
    row = lambda width: pl.BlockSpec((bm, width), lambda i, j: (i, 0))
    col = lambda depth: pl.BlockSpec((depth, bn), lambda i, j: (0, j))
    grid = (m // bm, n // bn)
    c_in, c_out, c_shape, c_args = _cast_plan(casts, grid)
    out, *casted = pl.pallas_call(
        _hosting(functools.partial(_merge_kernel, inv_d=1.0 / d), 10, 1, len(casts)),
        grid=grid,
        in_specs=[row(d), row(LANES), row(ro.shape[1]), row(fo.shape[1]), row(mo.shape[1]),
                  pl.BlockSpec((3, d, bn), lambda i, j: (0, 0, j)),
                  pl.BlockSpec((None, 3, bn), lambda i, j: (layer, 0, j)),
                  col(ro.shape[1]), col(fo.shape[1]), col(mo.shape[1])] + c_in,
        out_specs=[pl.BlockSpec((bm, bn), lambda i, j: (i, j))] + c_out,
        out_shape=[jax.ShapeDtypeStruct((m, n), BF16)] + c_shape,
        compiler_params=_params("parallel", "parallel"),
        name="gated_merge",
    )(u, ssq, ro, fo, mo, w_gate, b_gate, w_up_ret, w_up_fox, w_up_mla, *c_args)
    return out, casted


def _rope_table_kernel(pos_ref, inv_ref, coef_ref, rc_ref, rs_ref, mc_ref, ma_ref, mb_ref):
    p = pos_ref[...].astype(F32)
    ang_r = p * inv_ref[0:1, :]
    ang_m = p * inv_ref[1:2, :]
    rc_ref[...] = jnp.cos(ang_r)
    rs_ref[...] = jnp.sin(ang_r) * coef_ref[0:1, :]
    sin_m = jnp.sin(ang_m)
    mc_ref[...] = jnp.cos(ang_m) * coef_ref[1:2, :]
    ma_ref[...] = sin_m * coef_ref[2:3, :]
    mb_ref[...] = sin_m * coef_ref[3:4, :]


def rope_tables(positions):
    t = positions.size
    h_r, h_m = RET_DK // 2, MLA_ROPE // 2
    inv_r = ROPE_BASE ** (-jnp.arange(h_r, dtype=F32) / h_r)
    inv_m = ROPE_BASE ** (-jnp.arange(h_m, dtype=F32) / h_m)
    zeros_m = jnp.zeros((LANES - 2 * h_m,), F32)
    ones_m = jnp.ones((h_m,), F32)
    inv = jnp.stack([jnp.concatenate([inv_r, inv_r]),
                     jnp.concatenate([inv_m, inv_m, zeros_m])])
    coef = jnp.stack([
        jnp.concatenate([-jnp.ones((h_r,), F32), jnp.ones((h_r,), F32)]),
        jnp.concatenate([ones_m, ones_m, zeros_m]),
        jnp.concatenate([-ones_m, 0 * ones_m, zeros_m]),
        jnp.concatenate([0 * ones_m, ones_m, zeros_m]),
    ])
    bm = _tile(t, 1024, SUBLANES)
    tab = jax.ShapeDtypeStruct((t, LANES), F32)
    return pl.pallas_call(
        _rope_table_kernel,
        grid=(t // bm,),
        in_specs=[pl.BlockSpec((bm, 1), lambda i: (i, 0)),
                  pl.BlockSpec((2, LANES), lambda i: (0, 0)),
                  pl.BlockSpec((4, LANES), lambda i: (0, 0))],
        out_specs=[pl.BlockSpec((bm, LANES), lambda i: (i, 0))] * 5,
        out_shape=[tab] * 5,
        compiler_params=_params("parallel"),
        name="rope_tables",
    )(positions.reshape(t, 1), inv, coef)


def _ret_log_gamma(h):
    return math.log1p(-2.0 ** (-5.0 - h))


def _retention_kernel(q_ref, k_ref, v_ref, g_ref, rc_ref, rs_ref, norm_ref, o_ref, state_ref, decay_ref,
                      *, blk):
    first = jnp.logical_and(pl.program_id(0) == 0, pl.program_id(1) == 0)

    @pl.when(first)
    def _():
        row = lax.broadcasted_iota(jnp.int32, (blk, blk), 0)
        col = lax.broadcasted_iota(jnp.int32, (blk, blk), 1)
        dist = jnp.abs(row - col).astype(F32)
        visible = (col // CHUNK) <= (row // CHUNK)
        for h in range(RET_HEADS):
            decay_ref[h] = jnp.where(visible, jnp.exp(_ret_log_gamma(h) * dist), 0.0)

    @pl.when(pl.program_id(1) == 0)
    def _():
        state_ref[...] = jnp.zeros_like(state_ref)

    rc = rc_ref[0]
    rs = rs_ref[0]
    idx = lax.broadcasted_iota(jnp.int32, (blk, 1), 0).astype(F32)
    for h in range(RET_HEADS):
        lg = _ret_log_gamma(h)
        qk = slice(h * RET_DK, (h + 1) * RET_DK)
        vv = slice(h * RET_DV, (h + 1) * RET_DV)
        q = q_ref[0, :, qk]
        k = k_ref[0, :, qk]
        q = (q * rc + pltpu.roll(q, RET_DK // 2, 1) * rs) * (RET_DK ** -0.5)
        k = k * rc + pltpu.roll(k, RET_DK // 2, 1) * rs
        v = v_ref[0, :, vv]
        state = state_ref[h]
        scores = _dot_nt(q.astype(BF16), k.astype(BF16)) * decay_ref[h]
        q_in = (q * jnp.exp(lg * (idx + 1.0))).astype(BF16)
        out = _dot(scores.astype(BF16), v) + _dot(q_in, state.astype(BF16))
        k_out = (k * jnp.exp(lg * (blk - 1.0 - idx))).astype(BF16)
        state_ref[h] = state * math.exp(lg * blk) + _dot_tn(k_out, v)
        out = out * lax.rsqrt(jnp.mean(out * out, axis=-1, keepdims=True) + EPS)
        gate = g_ref[0, :, vv].astype(F32)
        out = out * norm_ref[:, vv] * (gate * jax.nn.sigmoid(gate))
        o_ref[0, :, vv] = out.astype(o_ref.dtype)


def retention(za, zb, rc, rs, ret_norm, *, blk=256):
    b, s, _ = za.shape
    blk = _tile(s, blk, CHUNK)
    qw, vw = RET_HEADS * RET_DK, RET_HEADS * RET_DV
    return pl.pallas_call(
        functools.partial(_retention_kernel, blk=blk),
        grid=(b, s // blk),
        in_specs=[pl.BlockSpec((1, blk, qw), lambda i, j: (i, j, A_RQ // qw)),
                  pl.BlockSpec((1, blk, qw), lambda i, j: (i, j, A_RK // qw)),
                  pl.BlockSpec((1, blk, vw), lambda i, j: (i, j, B_RV // vw)),
                  pl.BlockSpec((1, blk, vw), lambda i, j: (i, j, B_RG // vw)),
                  pl.BlockSpec((1, blk, LANES), lambda i, j: (i, j, 0)),
                  pl.BlockSpec((1, blk, LANES), lambda i, j: (i, j, 0)),
                  pl.BlockSpec((1, vw), lambda i, j: (0, 0))],
        out_specs=pl.BlockSpec((1, blk, vw), lambda i, j: (i, j, 0)),
        out_shape=jax.ShapeDtypeStruct((b, s, vw), BF16),
        scratch_shapes=[pltpu.VMEM((RET_HEADS, RET_DK, RET_DV), F32),
                        pltpu.VMEM((RET_HEADS, blk, blk), F32)],
        compiler_params=_params("arbitrary", "arbitrary"),
        name="retention",
    )(za, za, zb, zb, rc, rs, ret_norm.reshape(1, vw))


def _split3(x):
    hi = x.astype(BF16).astype(F32)
    rest = x - hi
    mid = rest.astype(BF16).astype(F32)
    lo = (rest - mid).astype(BF16).astype(F32)
    return hi, mid, lo


def _forget_cumsum_kernel(z_ref, bias_ref, qa_ref, ka_ref, carry_ref, *, rows):
    @pl.when(pl.program_id(1) == 0)
    def _():
        carry_ref[...] = jnp.zeros_like(carry_ref)

    lane = lax.broadcasted_iota(jnp.int32, (rows, LANES), 1)
    is_gate = jnp.logical_and(lane >= FF_LANE, lane < FF_LANE + FOX_HEADS)
    x = z_ref[0] + bias_ref[...]
    log_f = jnp.minimum(x, 0.0) - jnp.log(1.0 + jnp.exp(-jnp.abs(x)))
    log_f = jnp.where(is_gate, log_f * LOG2_E, 0.0)
    tri = (lax.broadcasted_iota(jnp.int32, (rows, rows), 0)
           >= lax.broadcasted_iota(jnp.int32, (rows, rows), 1)).astype(BF16)
    cum = carry_ref[...]
    for piece in _split3(log_f):
        cum = cum + _dot(tri, piece.astype(BF16))
    carry_ref[...] = cum[rows - 1:rows, :]
    one = jnp.where(lane < 2 * N_SPLIT, 1.0, 0.0)
    for h in range(FOX_HEADS):
        col = jnp.sum(jnp.where(lane == FF_LANE + h, cum, 0.0), axis=1, keepdims=True)
        pieces = _split3(jnp.broadcast_to(col, (rows, LANES)))
        qa = jnp.where(lane < N_SPLIT, 0.0, one)
        ka = jnp.where(lane < N_SPLIT, one, 0.0)
        for i, piece in enumerate(pieces):
            qa = jnp.where(lane == i, piece, qa)
            ka = jnp.where(lane == N_SPLIT + i, -piece, ka)
        qa_ref[0, :, h * LANES:(h + 1) * LANES] = qa.astype(qa_ref.dtype)
        ka_ref[0, :, h * LANES:(h + 1) * LANES] = ka.astype(ka_ref.dtype)


def forget_bias_lanes(za, b_forget, *, rows=256):
    b, s, _ = za.shape
    rows = _tile(s, rows)
    bias = jnp.zeros((1, LANES), F32).at[0, FF_LANE:FF_LANE + FOX_HEADS].set(b_forget)
    aux = jax.ShapeDtypeStruct((b, s, FOX_HEADS * LANES), BF16)
    return pl.pallas_call(
        functools.partial(_forget_cumsum_kernel, rows=rows),
        grid=(b, s // rows),
        in_specs=[pl.BlockSpec((1, rows, LANES), lambda i, j: (i, j, A_KR // LANES)),
                  pl.BlockSpec((1, LANES), lambda i, j: (0, 0))],
        out_specs=[pl.BlockSpec((1, rows, FOX_HEADS * LANES), lambda i, j: (i, j, 0))] * 2,
        out_shape=[aux, aux],
        scratch_shapes=[pltpu.VMEM((1, LANES), F32)],
        compiler_params=_params("arbitrary", "arbitrary"),
        name="forget_bias_lanes",
    )(za, bias)


def _attn_kernel(qi_ref, ki_ref, *refs, has_aux, chunked_mask, hp, dk, dv, t):
    if has_aux:
        q_ref, qa_ref, k_ref, ka_ref, v_ref, o_ref, m_ref, l_ref, acc_ref = refs
    else:
        q_ref, k_ref, v_ref, o_ref, m_ref, l_ref, acc_ref = refs
    pair = pl.program_id(2)
    qi, ki = qi_ref[pair], ki_ref[pair]

    @pl.when(ki == 0)
    def _():
        m_ref[...] = jnp.full_like(m_ref, NEG_BIG)
        l_ref[...] = jnp.zeros_like(l_ref)
        acc_ref[...] = jnp.zeros_like(acc_ref)

    def step(diagonal):
        if diagonal:
            row = lax.broadcasted_iota(jnp.int32, (t, t), 0)
            col = lax.broadcasted_iota(jnp.int32, (t, t), 1)
            keep = (col // CHUNK) <= (row // CHUNK) if chunked_mask else col <= row
        for h in range(hp):
            q = q_ref[0, :, h * dk:(h + 1) * dk]
            k = k_ref[0, :, h * dk:(h + 1) * dk]
            if has_aux:
                q = jnp.concatenate([q, qa_ref[0, :, h * LANES:(h + 1) * LANES]], axis=1)
                k = jnp.concatenate([k, ka_ref[0, :, h * LANES:(h + 1) * LANES]], axis=1)
            s = _dot_nt(q, k)
            if diagonal:
                s = jnp.where(keep, s, NEG_BIG)
            m_old = m_ref[h]
            m_new = jnp.maximum(m_old, jnp.max(s, axis=1, keepdims=True))
            alpha = jnp.exp2(m_old - m_new)
            p = jnp.exp2(s - jnp.concatenate([m_new] * (t // LANES), axis=1))
            l_ref[h] = alpha * l_ref[h] + jnp.sum(p, axis=1, keepdims=True)
            acc_ref[h] = alpha * acc_ref[h] + _dot(p.astype(BF16), v_ref[0, :, h * dv:(h + 1) * dv])
            m_ref[h] = m_new

    @pl.when(ki < qi)
    def _():
        step(False)

    @pl.when(ki == qi)
    def _():
        step(True)
        for h in range(hp):
            o_ref[0, :, h * dv:(h + 1) * dv] = (acc_ref[h] / l_ref[h]).astype(o_ref.dtype)


def attention(q_arr, q_col, k_arr, k_col, v_arr, v_col, *, heads, dk, dv, aux=None, chunked_mask,
              tile=512, hp=ATTN_HEADS_PER_STEP):
    assert dv == LANES and heads % hp == 0
    b, s, _ = q_arr.shape
    t = _tile(s, tile)
    n = s // t
    pairs = [(i, j) for i in range(n) for j in range(i + 1)]
    qi_tab = jnp.asarray([p[0] for p in pairs], jnp.int32)
    ki_tab = jnp.asarray([p[1] for p in pairs], jnp.int32)

    def spec(width, col, table):
        assert col % (hp * width) == 0
        first = col // (hp * width)
        if table == "q":
            return pl.BlockSpec((1, t, hp * width), lambda i, g, p, qt, kt: (i, qt[p], first + g))
        return pl.BlockSpec((1, t, hp * width), lambda i, g, p, qt, kt: (i, kt[p], first + g))

    if aux is None:
        in_specs = [spec(dk, q_col, "q"), spec(dk, k_col, "k"), spec(dv, v_col, "k")]
        args = [q_arr, k_arr, v_arr]
    else:
        in_specs = [spec(dk, q_col, "q"), spec(LANES, 0, "q"), spec(dk, k_col, "k"), spec(LANES, 0, "k"),
                    spec(dv, v_col, "k")]
        args = [q_arr, aux[0], k_arr, aux[1], v_arr]
    stat = pltpu.VMEM((hp, t, LANES), F32)
    return pl.pallas_call(
        functools.partial(_attn_kernel, has_aux=aux is not None, chunked_mask=chunked_mask, hp=hp, dk=dk,
                          dv=dv, t=t),
        grid_spec=pltpu.PrefetchScalarGridSpec(
            num_scalar_prefetch=2,
            grid=(b, heads // hp, len(pairs)),
            in_specs=in_specs,
            out_specs=spec(dv, 0, "q"),
            scratch_shapes=[stat, stat, pltpu.VMEM((hp, t, dv), F32)]),
        out_shape=jax.ShapeDtypeStruct((b, s, heads * dv), BF16),
        compiler_params=_params("parallel", "parallel", "arbitrary"),
        name="fox_attention" if aux is not None else "mla_attention",
    )(qi_tab, ki_tab, *args)


def _mla_rope(x, mc, ma, mb):
    return x * mc + pltpu.roll(x, LANES - MLA_ROPE // 2, 1) * ma + pltpu.roll(x, MLA_ROPE // 2, 1) * mb


def _mla_q_kernel(c_ref, g_ref, w_ref, mc_ref, ma_ref, mb_ref, o_ref):
    x = c_ref[...]
    xn = (x * lax.rsqrt(jnp.mean(x * x, axis=-1, keepdims=True) + EPS) * g_ref[...]).astype(BF16)
    q = _dot(xn, w_ref[...])
    scale = (MLA_NOPE + MLA_ROPE) ** -0.5 * LOG2_E
    mc, ma, mb = mc_ref[...], ma_ref[...], mb_ref[...]
    for h in range(MLA_HEADS):
        lo = h * MLA_QK_PAD
        o_ref[:, lo:lo + MLA_NOPE] = (q[:, lo:lo + MLA_NOPE] * scale).astype(o_ref.dtype)
        pe = _mla_rope(q[:, lo + MLA_NOPE:lo + MLA_QK_PAD], mc, ma, mb)
        o_ref[:, lo + MLA_NOPE:lo + MLA_QK_PAD] = (pe * scale).astype(o_ref.dtype)


def mla_queries(za2, q_norm, w_uq_pad, layer, mc, ma, mb, *, bm=512):
    t = za2.shape[0]
    bm = _tile(t, bm, SUBLANES)
    tab = pl.BlockSpec((bm, LANES), lambda i: (i, 0))
    return pl.pallas_call(
        _mla_q_kernel,
        grid=(t // bm,),
        in_specs=[pl.BlockSpec((bm, MLA_Q_LORA), lambda i: (i, A_CQ // MLA_Q_LORA)),
                  pl.BlockSpec((1, MLA_Q_LORA), lambda i: (0, 0)),
                  pl.BlockSpec((None,) + w_uq_pad.shape[1:], lambda i: (layer, 0, 0)),
                  tab, tab, tab],
        out_specs=pl.BlockSpec((bm, MLA_HEADS * MLA_QK_PAD), lambda i: (i, 0)),
        out_shape=jax.ShapeDtypeStruct((t, MLA_HEADS * MLA_QK_PAD), BF16),
        compiler_params=_params("parallel"),
        name="mla_queries",
    )(za2, q_norm.reshape(1, -1), w_uq_pad, mc, ma, mb)


def _mla_kv_kernel(c_ref, kr_ref, g_ref, wk_ref, wv_ref, mc_ref, ma_ref, mb_ref, k_ref, v_ref):
    x = c_ref[...]
    xn = (x * lax.rsqrt(jnp.mean(x * x, axis=-1, keepdims=True) + EPS) * g_ref[...]).astype(BF16)
    k_nope = _dot(xn, wk_ref[...])
    v_ref[...] = _dot(xn, wv_ref[...]).astype(v_ref.dtype)
    k_pe = _mla_rope(kr_ref[...], mc_ref[...], ma_ref[...], mb_ref[...]).astype(k_ref.dtype)
    for h in range(MLA_HEADS):
        lo = h * MLA_QK_PAD
        k_ref[:, lo:lo + MLA_NOPE] = k_nope[:, h * MLA_NOPE:(h + 1) * MLA_NOPE].astype(k_ref.dtype)
        k_ref[:, lo + MLA_NOPE:lo + MLA_QK_PAD] = k_pe


def mla_keys_values(za2, kv_norm, wk, wv, layer, mc, ma, mb, *, bm=512):
    t = za2.shape[0]
    bm = _tile(t, bm, SUBLANES)
    tab = pl.BlockSpec((bm, LANES), lambda i: (i, 0))
    w_spec = pl.BlockSpec((None,) + wk.shape[1:], lambda i: (layer, 0, 0))
    return pl.pallas_call(
        _mla_kv_kernel,
        grid=(t // bm,),
        in_specs=[pl.BlockSpec((bm, MLA_KV_LORA), lambda i: (i, A_CKV // MLA_KV_LORA)),
                  pl.BlockSpec((bm, LANES), lambda i: (i, A_KR // LANES)),
                  pl.BlockSpec((1, MLA_KV_LORA), lambda i: (0, 0)),
                  w_spec, w_spec, tab, tab, tab],
        out_specs=[pl.BlockSpec((bm, MLA_HEADS * MLA_QK_PAD), lambda i: (i, 0)),
                   pl.BlockSpec((bm, MLA_HEADS * MLA_DV), lambda i: (i, 0))],
        out_shape=[jax.ShapeDtypeStruct((t, MLA_HEADS * MLA_QK_PAD), BF16),
                   jax.ShapeDtypeStruct((t, MLA_HEADS * MLA_DV), BF16)],
        compiler_params=_params("parallel"),
        name="mla_keys_values",
    )(za2, za2, kv_norm.reshape(1, -1), wk, wv, mc, ma, mb)


def _prepare_weights(p):
    w_in_t = jnp.swapaxes(p["w_in"], 1, 2)
    depth, width, d = w_in_t.shape
    rq_rk_end = 2 * RET_HEADS * RET_DK
    bf_end = rq_rk_end + B_WIDTH
    ff_end = bf_end + FOX_HEADS
    fq0 = rq_rk_end + B_FQ
    row_scale = jnp.ones((width, 1), F32).at[fq0:fq0 + FOX_HEADS * FOX_DH].set(FOX_DH ** -0.5 * LOG2_E)
    w_in_b = (w_in_t * row_scale).astype(BF16)
    used = rq_rk_end + (width - ff_end) + FOX_HEADS
    w_in_a = jnp.concatenate([w_in_b[:, :rq_rk_end], w_in_b[:, ff_end:], w_in_b[:, bf_end:ff_end],
                              jnp.zeros((depth, A_WIDTH - used, d), BF16)], axis=1)
    w_uq = p["w_uq"].reshape(depth, MLA_Q_LORA, MLA_HEADS, MLA_NOPE + MLA_ROPE)
    w_uq = jnp.pad(w_uq, ((0, 0), (0, 0), (0, 0), (0, MLA_QK_PAD - MLA_NOPE - MLA_ROPE)))
    w_ukv = p["w_ukv"].reshape(depth, MLA_KV_LORA, MLA_HEADS, MLA_NOPE + MLA_DV)
    out = dict(
        w13_first=p["ffn1_w13"][:1].astype(BF16),
        w_in_a=w_in_a, w_in_b=w_in_b, w_in_b_col=rq_rk_end,
        w_uq=w_uq.reshape(depth, MLA_Q_LORA, MLA_HEADS * MLA_QK_PAD).astype(BF16),
        w_uk=w_ukv[..., :MLA_NOPE].reshape(depth, MLA_KV_LORA, MLA_HEADS * MLA_NOPE).astype(BF16),
        w_uv=w_ukv[..., MLA_NOPE:].reshape(depth, MLA_KV_LORA, MLA_HEADS * MLA_DV).astype(BF16))
    return out


def _ffn(h, u, ssq, w13, w13_layer, w2_f32, layer, next_gain, down_casts):
    g, (w2,) = swiglu_up(u, ssq, w13, w13_layer, casts=[(w2_f32, layer)])
    return residual_matmul(g, w2[None], 0, h, scale=0.5, gain=next_gain, casts=down_casts, bm=512, bn=512,
                           name="ffn_down")


def _mixer(h, u, ssq, batch, tables, p, w, w_gate, layer, next_gain, out_casts):
    rc, rs, mc, ma, mb = tables
    t, d = h.shape
    s = t // batch
    za, (w_up_ret, w_up_fox, w_up_mla) = normed_matmul(
        u, ssq, w["w_in_a"], layer, out_dtype=F32, name="in_proj_f32",
        casts=[(p["w_up_ret"], layer), (p["w_up_fox"], layer), (p["w_up_mla"], layer)])
    zb, _ = normed_matmul(u, ssq, w["w_in_b"], layer, col=w["w_in_b_col"], n=B_WIDTH, out_dtype=BF16,
                          name="in_proj_bf16")
    za3 = za.reshape(batch, s, A_WIDTH)
    zb3 = zb.reshape(batch, s, B_WIDTH)
    tab3 = lambda x: x.reshape(batch, s, LANES)

    ro = retention(za3, zb3, tab3(rc), tab3(rs), p["ret_norm"][layer])
    fo = attention(zb3, B_FQ, zb3, B_FK, zb3, B_FV, heads=FOX_HEADS, dk=FOX_DH, dv=FOX_DH,
                   aux=forget_bias_lanes(za3, p["b_forget"][layer]), chunked_mask=False)
    qm = mla_queries(za, p["mla_q_norm"][layer], w["w_uq"], layer, mc, ma, mb)
    km, vm = mla_keys_values(za, p["mla_kv_norm"][layer], w["w_uk"], w["w_uv"], layer, mc, ma, mb)
    mo = attention(qm.reshape(batch, s, -1), 0, km.reshape(batch, s, -1), 0, vm.reshape(batch, s, -1), 0,
                   heads=MLA_HEADS, dk=MLA_QK_PAD, dv=MLA_DV, chunked_mask=True)

    merged, (w_out, *merge_done) = gated_merge(
        u, ssq, ro.reshape(t, -1), fo.reshape(t, -1), mo.reshape(t, -1), w_gate, p["b_gate"], layer,
        w_up_ret, w_up_fox, w_up_mla, casts=[(p["w_out"], layer), *out_casts[:1]])
    result, out_done = residual_matmul(merged, w_out[None], 0, h, scale=1.0, gain=next_gain, bn=512,
                                       casts=out_casts[1:], name="out_proj")
    return result, merge_done + out_done


def kernel(x, positions, ffn1_norm, ffn1_w13, ffn1_w2, mix_norm, w_in, b_forget, ret_norm, mla_q_norm,
           mla_kv_norm, w_uq, w_ukv, w_up_ret, w_up_fox, w_up_mla, w_gate, b_gate, w_out, ffn2_norm,
           ffn2_w13, ffn2_w2, final_norm):
    p = dict(mix_norm=mix_norm, w_in=w_in, b_forget=b_forget, ret_norm=ret_norm, mla_q_norm=mla_q_norm,
             mla_kv_norm=mla_kv_norm, w_uq=w_uq, w_ukv=w_ukv, w_up_ret=w_up_ret, w_up_fox=w_up_fox,
             w_up_mla=w_up_mla, w_gate=w_gate, b_gate=b_gate, w_out=w_out, ffn1_w13=ffn1_w13,
             ffn1_w2=ffn1_w2, ffn2_w13=ffn2_w13, ffn2_w2=ffn2_w2)
    batch, s, d = x.shape
    h = x.reshape(batch * s, d)
    w = _prepare_weights(p)
    tables = rope_tables(positions)
    depth = ffn1_norm.shape[0]
    u, ssq = prenorm(h, ffn1_norm[0])
    w_gate_rows = w_gate.reshape(depth, 3 * d, d)
    w13, w13_layer = w["w13_first"], 0
    for layer in range(depth):
        last = layer + 1 == depth
        (h, u, ssq), (gate,) = _ffn(h, u, ssq, w13, w13_layer, ffn1_w2, layer, mix_norm[layer],
                                    [(w_gate_rows, layer)])
        (h, u, ssq), (w13_b, w2_b) = _mixer(h, u, ssq, batch, tables, p, w, gate.reshape(3, d, d), layer,
                                            ffn2_norm[layer], [(ffn2_w13, layer), (ffn2_w2, layer)])
        g, next_w13 = swiglu_up(u, ssq, w13_b[None], 0, casts=[] if last else [(ffn1_w13, layer + 1)])
        result, _ = residual_matmul(g, w2_b[None], 0, h, scale=0.5, gain=None if last else ffn1_norm[layer + 1],
                                    bm=512, bn=512, name="ffn_down")
        if last:
            h = result
        else:
            h, u, ssq = result
            w13, w13_layer = next_w13[0][None], 0
    return rmsnorm(h, final_norm, x.dtype).reshape(batch, s, d)
```

```python
import functools
import math

import jax
import jax.numpy as jnp
from jax import lax
from jax.experimental import pallas as pl
from jax.experimental.pallas import tpu as pltpu

F32 = jnp.float32
BF16 = jnp.bfloat16

CHUNK = 64
RET_HEADS, RET_DK, RET_DV = 8, 128, 256
FOX_HEADS, FOX_DH = 8, 128
MLA_HEADS, MLA_Q_LORA, MLA_KV_LORA = 8, 1024, 512
MLA_NOPE, MLA_ROPE, MLA_DV = 128, 64, 128
MLA_QK_PAD = 256
ROPE_BASE = 10000.0
EPS = 1e-6
NEG_BIG = -1e30
LOG2_E = math.log2(math.e)
N_SPLIT = 3
ATTN_HEADS_PER_STEP = 8

LANES = 128
SUBLANES = 8
BF16_ROWS = 2 * SUBLANES
VMEM_LIMIT_BYTES = 62 * 1024 * 1024

A_RQ, A_RK, A_CQ, A_CKV, A_KR = 0, 1024, 2048, 3072, 3584
A_FF = A_KR + MLA_ROPE
A_WIDTH = 4096
FF_LANE = A_FF - A_KR
B_RV, B_RG, B_FQ, B_FK, B_FV = 0, 2048, 4096, 5120, 6144
B_WIDTH = 7168


def _tile(dim, pref, quantum=LANES):
    if dim <= pref:
        return dim
    t = (pref // quantum) * quantum
    while t > quantum and dim % t:
        t -= quantum
    assert dim % t == 0, (dim, pref)
    return t


def _params(*sem):
    return pltpu.CompilerParams(dimension_semantics=sem, vmem_limit_bytes=VMEM_LIMIT_BYTES)


def _dot(a, b):
    return jnp.dot(a, b, preferred_element_type=F32)


def _dot_nt(a, b):
    return lax.dot_general(a, b, (((1,), (1,)), ((), ())), preferred_element_type=F32)


def _dot_tn(a, b):
    return lax.dot_general(a, b, (((0,), (0,)), ((), ())), preferred_element_type=F32)


def _cast_plan(casts, grid):
    steps = math.prod(grid)

    def linear(ids):
        idx = ids[0]
        for extent, i in zip(grid[1:], ids[1:]):
            idx = idx * extent + i
        return idx

    in_specs, out_specs, out_shapes, operands = [], [], [], []
    for stack, layer in casts:
        _, rows, cols = stack.shape
        blocks = max(n for n in range(1, steps + 1)
                     if steps % n == 0 and rows % n == 0 and (rows // n) % BF16_ROWS == 0)
        rb, rep = rows // blocks, steps // blocks
        in_specs.append(pl.BlockSpec((None, rb, cols),
                                     lambda *ids, layer=layer, rep=rep: (layer, linear(ids) // rep, 0)))
        out_specs.append(pl.BlockSpec((rb, cols), lambda *ids, rep=rep: (linear(ids) // rep, 0)))
        out_shapes.append(jax.ShapeDtypeStruct((rows, cols), BF16))
        operands.append(stack)
    return in_specs, out_specs, out_shapes, operands


def _hosting(body, n_in, n_out, n_cast):
    if not n_cast:
        return body

    def kernel(*refs):
        ins, rest = refs[:n_in], refs[n_in:]
        srcs, rest = rest[:n_cast], rest[n_cast:]
        outs, rest = rest[:n_out], rest[n_out:]
        dsts, scratch = rest[:n_cast], rest[n_cast:]
        body(*ins, *outs, *scratch)
        for src, dst in zip(srcs, dsts):
            dst[...] = src[...].astype(dst.dtype)

    return kernel


def _rmsnorm_kernel(x_ref, g_ref, o_ref):
    x = x_ref[...]
    ms = jnp.mean(x * x, axis=-1, keepdims=True)
    o_ref[...] = (x * lax.rsqrt(ms + EPS) * g_ref[...]).astype(o_ref.dtype)


def rmsnorm(x, g, out_dtype):
    m, d = x.shape
    bm = _tile(m, 512, SUBLANES)
    return pl.pallas_call(
        _rmsnorm_kernel,
        grid=(m // bm,),
        in_specs=[pl.BlockSpec((bm, d), lambda i: (i, 0)),
                  pl.BlockSpec((1, d), lambda i: (0, 0))],
        out_specs=pl.BlockSpec((bm, d), lambda i: (i, 0)),
        out_shape=jax.ShapeDtypeStruct((m, d), out_dtype),
        compiler_params=_params("parallel"),
        name="rmsnorm",
    )(x, g.reshape(1, d))


def _lane_partial_ssq(x):
    sq = x * x
    part = sq[:, :LANES]
    for c in range(1, x.shape[1] // LANES):
        part = part + sq[:, c * LANES:(c + 1) * LANES]
    return part


def _row_rstd(ssq_ref, inv_d):
    return lax.rsqrt(jnp.sum(ssq_ref[...], axis=1, keepdims=True) * inv_d + EPS)


def _prenorm_kernel(x_ref, g_ref, u_ref, ssq_ref):
    x = x_ref[...]
    u_ref[...] = (x * g_ref[...]).astype(u_ref.dtype)
    ssq_ref[...] = _lane_partial_ssq(x)


def prenorm(x, g):
    m, d = x.shape
    bm = _tile(m, 512, SUBLANES)
    return pl.pallas_call(
        _prenorm_kernel,
        grid=(m // bm,),
        in_specs=[pl.BlockSpec((bm, d), lambda i: (i, 0)),
                  pl.BlockSpec((1, d), lambda i: (0, 0))],
        out_specs=[pl.BlockSpec((bm, d), lambda i: (i, 0)),
                   pl.BlockSpec((bm, LANES), lambda i: (i, 0))],
        out_shape=[jax.ShapeDtypeStruct((m, d), BF16), jax.ShapeDtypeStruct((m, LANES), F32)],
        compiler_params=_params("parallel"),
        name="prenorm",
    )(x, g.reshape(1, d))


def _proj_kernel(a_ref, ssq_ref, wt_ref, o_ref, *, inv_d):
    acc = _dot_nt(a_ref[...], wt_ref[...])
    o_ref[...] = (acc * _row_rstd(ssq_ref, inv_d)).astype(o_ref.dtype)


def normed_matmul(a, ssq, wt, layer, *, col=0, n=None, out_dtype, casts=(), bm=1024, bn=1024, name):
    m, kd = a.shape
    n = wt.shape[1] if n is None else n
    bm, bn = _tile(m, bm, SUBLANES), _tile(math.gcd(n, col) if col else n, bn)
    first = col // bn
    grid = (m // bm, n // bn)
    c_in, c_out, c_shape, c_args = _cast_plan(casts, grid)
    out, *casted = pl.pallas_call(
        _hosting(functools.partial(_proj_kernel, inv_d=1.0 / kd), 3, 1, len(casts)),
        grid=grid,
        in_specs=[pl.BlockSpec((bm, kd), lambda i, j: (i, 0)),
                  pl.BlockSpec((bm, LANES), lambda i, j: (i, 0)),
                  pl.BlockSpec((None, bn, kd), lambda i, j: (layer, first + j, 0))] + c_in,
        out_specs=[pl.BlockSpec((bm, bn), lambda i, j: (i, j))] + c_out,
        out_shape=[jax.ShapeDtypeStruct((m, n), out_dtype)] + c_shape,
        compiler_params=_params("parallel", "parallel"),
        name=name,
    )(a, ssq, wt, *c_args)
    return out, casted


def _residual_kernel(a_ref, w_ref, r_ref, *refs, nk, scale, emit_norm):
    if emit_norm:
        g_ref, o_ref, u_ref, ssq_ref = refs
    else:
        o_ref, = refs

    def finish(h):
        o_ref[...] = h
        if emit_norm:
            u_ref[...] = (h * g_ref[...]).astype(u_ref.dtype)
            row_ssq = _lane_partial_ssq(h)
            j = pl.program_id(1)

            @pl.when(j == 0)
            def _():
                ssq_ref[...] = row_ssq

            @pl.when(j > 0)
            def _():
                ssq_ref[...] += row_ssq

    part = scale * _dot(a_ref[...], w_ref[...])
    if nk == 1:
        finish(r_ref[...] + part)
        return
    k = pl.program_id(2)

    @pl.when(k == 0)
    def _():
        o_ref[...] = r_ref[...] + part

    @pl.when(jnp.logical_and(k > 0, k < nk - 1))
    def _():
        o_ref[...] += part

    @pl.when(k == nk - 1)
    def _():
        finish(o_ref[...] + part)


def residual_matmul(a, w, layer, res, *, scale, gain=None, casts=(), bm=1024, bn=1024, bk=None, name):
    m, kd = a.shape
    n = w.shape[2]
    bm, bn = _tile(m, bm, SUBLANES), _tile(n, bn)
    bk = kd if bk is None else _tile(kd, bk)
    nk = kd // bk
    tile = pl.BlockSpec((bm, bn), lambda i, j, k: (i, j))
    in_specs = [pl.BlockSpec((bm, bk), lambda i, j, k: (i, k)),
                pl.BlockSpec((None, bk, bn), lambda i, j, k: (layer, k, j)),
                tile]
    args = [a, w, res]
    out_specs, out_shape = [tile], [jax.ShapeDtypeStruct((m, n), F32)]
    if gain is not None:
        in_specs.append(pl.BlockSpec((1, bn), lambda i, j, k: (0, j)))
        args.append(gain.reshape(1, n))
        out_specs += [tile, pl.BlockSpec((bm, LANES), lambda i, j, k: (i, 0))]
        out_shape += [jax.ShapeDtypeStruct((m, n), BF16), jax.ShapeDtypeStruct((m, LANES), F32)]
    n_out = len(out_specs)
    grid = (m // bm, n // bn, nk)
    c_in, c_out, c_shape, c_args = _cast_plan(casts, grid)
    outs = pl.pallas_call(
        _hosting(functools.partial(_residual_kernel, nk=nk, scale=scale, emit_norm=gain is not None),
                 len(args), n_out, len(casts)),
        grid=grid,
        in_specs=in_specs + c_in,
        out_specs=out_specs + c_out,
        out_shape=out_shape + c_shape,
        compiler_params=_params("parallel", "arbitrary", "arbitrary"),
        name=name,
    )(*args, *c_args)
    result = outs[0] if gain is None else tuple(outs[:n_out])
    return result, list(outs[n_out:])


def _swiglu_kernel(u_ref, ssq_ref, w1_ref, w3_ref, o_ref, *, inv_d):
    u = u_ref[...]
    rstd = _row_rstd(ssq_ref, inv_d)
    a = _dot(u, w1_ref[...]) * rstd
    b = _dot(u, w3_ref[...]) * rstd
    o_ref[...] = (a * jax.nn.sigmoid(a) * b).astype(o_ref.dtype)


def swiglu_up(u, ssq, w13, layer, *, casts=(), bm=1024, bn=512):
    m, d = u.shape
    f = w13.shape[2] // 2
    bm, bn = _tile(m, bm, SUBLANES), _tile(f, bn)
    nf = f // bn
    grid = (m // bm, nf)
    c_in, c_out, c_shape, c_args = _cast_plan(casts, grid)
    out, *casted = pl.pallas_call(
        _hosting(functools.partial(_swiglu_kernel, inv_d=1.0 / d), 4, 1, len(casts)),
        grid=grid,
        in_specs=[pl.BlockSpec((bm, d), lambda i, j: (i, 0)),
                  pl.BlockSpec((bm, LANES), lambda i, j: (i, 0)),
                  pl.BlockSpec((None, d, bn), lambda i, j: (layer, 0, j)),
                  pl.BlockSpec((None, d, bn), lambda i, j: (layer, 0, j + nf))] + c_in,
        out_specs=[pl.BlockSpec((bm, bn), lambda i, j: (i, j))] + c_out,
        out_shape=[jax.ShapeDtypeStruct((m, f), BF16)] + c_shape,
        compiler_params=_params("parallel", "parallel"),
        name="swiglu_up",
    )(u, ssq, w13, w13, *c_args)
    return out, casted


def _merge_kernel(u_ref, ssq_ref, ro_ref, fo_ref, mo_ref, wg_ref, bg_ref, wr_ref, wf_ref, wm_ref, o_ref, *,
                  inv_d):
    u = u_ref[...]
    rstd = _row_rstd(ssq_ref, inv_d)
    total = None
    for i, (x_ref, w_ref) in enumerate(((ro_ref, wr_ref), (fo_ref, wf_ref), (mo_ref, wm_ref))):
        gate = jax.nn.sigmoid(_dot(u, wg_ref[i]) * rstd + bg_ref[i:i + 1, :])
        term = gate * _dot(x_ref[...], w_ref[...])
        total = term if total is None else total + term
    o_ref[...] = total.astype(o_ref.dtype)


def gated_merge(u, ssq, ro, fo, mo, w_gate, b_gate, layer, w_up_ret, w_up_fox, w_up_mla, *, casts=(),
                bm=512, bn=512):
    m, d = u.shape
    n = w_gate.shape[-1]
    bm, bn = _tile(m, bm, SUBLANES), _tile(n, bn)
    row = lambda width: pl.BlockSpec((bm, width), lambda i, j: (i, 0))
    col = lambda depth: pl.BlockSpec((depth, bn), lambda i, j: (0, j))
    grid = (m // bm, n // bn)
    c_in, c_out, c_shape, c_args = _cast_plan(casts, grid)
    out, *casted = pl.pallas_call(
        _hosting(functools.partial(_merge_kernel, inv_d=1.0 / d), 10, 1, len(casts)),
        grid=grid,
        in_specs=[row(d), row(LANES), row(ro.shape[1]), row(fo.shape[1]), row(mo.shape[1]),
                  pl.BlockSpec((3, d, bn), lambda i, j: (0, 0, j)),
                  pl.BlockSpec((None, 3, bn), lambda i, j: (layer, 0, j)),
                  col(ro.shape[1]), col(fo.shape[1]), col(mo.shape[1])] + c_in,
        out_specs=[pl.BlockSpec((bm, bn), lambda i, j: (i, j))] + c_out,
        out_shape=[jax.ShapeDtypeStruct((m, n), BF16)] + c_shape,
        compiler_params=_params("parallel", "parallel"),
        name="gated_merge",
    )(u, ssq, ro, fo, mo, w_gate, b_gate, w_up_ret, w_up_fox, w_up_mla, *c_args)
    return out, casted


def _rope_table_kernel(pos_ref, inv_ref, coef_ref, rc_ref, rs_ref, mc_ref, ma_ref, mb_ref):
    p = pos_ref[...].astype(F32)
    ang_r = p * inv_ref[0:1, :]
    ang_m = p * inv_ref[1:2, :]
    rc_ref[...] = jnp.cos(ang_r)
    rs_ref[...] = jnp.sin(ang_r) * coef_ref[0:1, :]
    sin_m = jnp.sin(ang_m)
    mc_ref[...] = jnp.cos(ang_m) * coef_ref[1:2, :]
    ma_ref[...] = sin_m * coef_ref[2:3, :]
    mb_ref[...] = sin_m * coef_ref[3:4, :]


def rope_tables(positions):
    t = positions.size
    h_r, h_m = RET_DK // 2, MLA_ROPE // 2
    inv_r = ROPE_BASE ** (-jnp.arange(h_r, dtype=F32) / h_r)
    inv_m = ROPE_BASE ** (-jnp.arange(h_m, dtype=F32) / h_m)
    zeros_m = jnp.zeros((LANES - 2 * h_m,), F32)
    ones_m = jnp.ones((h_m,), F32)
    inv = jnp.stack([jnp.concatenate([inv_r, inv_r]),
                     jnp.concatenate([inv_m, inv_m, zeros_m])])
    coef = jnp.stack([
        jnp.concatenate([-jnp.ones((h_r,), F32), jnp.ones((h_r,), F32)]),
        jnp.concatenate([ones_m, ones_m, zeros_m]),
        jnp.concatenate([-ones_m, 0 * ones_m, zeros_m]),
        jnp.concatenate([0 * ones_m, ones_m, zeros_m]),
    ])
    bm = _tile(t, 1024, SUBLANES)
    tab = jax.ShapeDtypeStruct((t, LANES), F32)
    return pl.pallas_call(
        _rope_table_kernel,
        grid=(t // bm,),
        in_specs=[pl.BlockSpec((bm, 1), lambda i: (i, 0)),
                  pl.BlockSpec((2, LANES), lambda i: (0, 0)),
                  pl.BlockSpec((4, LANES), lambda i: (0, 0))],
        out_specs=[pl.BlockSpec((bm, LANES), lambda i: (i, 0))] * 5,
        out_shape=[tab] * 5,
        compiler_params=_params("parallel"),
        name="rope_tables",
    )(positions.reshape(t, 1), inv, coef)


def _ret_log_gamma(h):
    return math.log1p(-2.0 ** (-5.0 - h))


def _retention_kernel(q_ref, k_ref, v_ref, g_ref, rc_ref, rs_ref, norm_ref, o_ref, state_ref, decay_ref,
                      *, blk):
    first = jnp.logical_and(pl.program_id(0) == 0, pl.program_id(1) == 0)

    @pl.when(first)
    def _():
        row = lax.broadcasted_iota(jnp.int32, (blk, blk), 0)
        col = lax.broadcasted_iota(jnp.int32, (blk, blk), 1)
        dist = jnp.abs(row - col).astype(F32)
        visible = (col // CHUNK) <= (row // CHUNK)
        for h in range(RET_HEADS):
            decay_ref[h] = jnp.where(visible, jnp.exp(_ret_log_gamma(h) * dist), 0.0)

    @pl.when(pl.program_id(1) == 0)
    def _():
        state_ref[...] = jnp.zeros_like(state_ref)

    rc = rc_ref[0]
    rs = rs_ref[0]
    idx = lax.broadcasted_iota(jnp.int32, (blk, 1), 0).astype(F32)
    for h in range(RET_HEADS):
        lg = _ret_log_gamma(h)
        qk = slice(h * RET_DK, (h + 1) * RET_DK)
        vv = slice(h * RET_DV, (h + 1) * RET_DV)
        q = q_ref[0, :, qk]
        k = k_ref[0, :, qk]
        q = (q * rc + pltpu.roll(q, RET_DK // 2, 1) * rs) * (RET_DK ** -0.5)
        k = k * rc + pltpu.roll(k, RET_DK // 2, 1) * rs
        v = v_ref[0, :, vv]
        state = state_ref[h]
        scores = _dot_nt(q.astype(BF16), k.astype(BF16)) * decay_ref[h]
        q_in = (q * jnp.exp(lg * (idx + 1.0))).astype(BF16)
        out = _dot(scores.astype(BF16), v) + _dot(q_in, state.astype(BF16))
        k_out = (k * jnp.exp(lg * (blk - 1.0 - idx))).astype(BF16)
        state_ref[h] = state * math.exp(lg * blk) + _dot_tn(k_out, v)
        out = out * lax.rsqrt(jnp.mean(out * out, axis=-1, keepdims=True) + EPS)
        gate = g_ref[0, :, vv].astype(F32)
        out = out * norm_ref[:, vv] * (gate * jax.nn.sigmoid(gate))
        o_ref[0, :, vv] = out.astype(o_ref.dtype)


def retention(za, zb, rc, rs, ret_norm, *, blk=256):
    b, s, _ = za.shape
    blk = _tile(s, blk, CHUNK)
    qw, vw = RET_HEADS * RET_DK, RET_HEADS * RET_DV
    return pl.pallas_call(
        functools.partial(_retention_kernel, blk=blk),
        grid=(b, s // blk),
        in_specs=[pl.BlockSpec((1, blk, qw), lambda i, j: (i, j, A_RQ // qw)),
                  pl.BlockSpec((1, blk, qw), lambda i, j: (i, j, A_RK // qw)),
                  pl.BlockSpec((1, blk, vw), lambda i, j: (i, j, B_RV // vw)),
                  pl.BlockSpec((1, blk, vw), lambda i, j: (i, j, B_RG // vw)),
                  pl.BlockSpec((1, blk, LANES), lambda i, j: (i, j, 0)),
                  pl.BlockSpec((1, blk, LANES), lambda i, j: (i, j, 0)),
                  pl.BlockSpec((1, vw), lambda i, j: (0, 0))],
        out_specs=pl.BlockSpec((1, blk, vw), lambda i, j: (i, j, 0)),
        out_shape=jax.ShapeDtypeStruct((b, s, vw), BF16),
        scratch_shapes=[pltpu.VMEM((RET_HEADS, RET_DK, RET_DV), F32),
                        pltpu.VMEM((RET_HEADS, blk, blk), F32)],
        compiler_params=_params("arbitrary", "arbitrary"),
        name="retention",
    )(za, za, zb, zb, rc, rs, ret_norm.reshape(1, vw))


def _split3(x):
    hi = x.astype(BF16).astype(F32)
    rest = x - hi
    mid = rest.astype(BF16).astype(F32)
    lo = (rest - mid).astype(BF16).astype(F32)
    return hi, mid, lo


def _forget_cumsum_kernel(z_ref, bias_ref, qa_ref, ka_ref, carry_ref, *, rows):
    @pl.when(pl.program_id(1) == 0)
    def _():
        carry_ref[...] = jnp.zeros_like(carry_ref)

    lane = lax.broadcasted_iota(jnp.int32, (rows, LANES), 1)
    is_gate = jnp.logical_and(lane >= FF_LANE, lane < FF_LANE + FOX_HEADS)
    x = z_ref[0] + bias_ref[...]
    log_f = jnp.minimum(x, 0.0) - jnp.log(1.0 + jnp.exp(-jnp.abs(x)))
    log_f = jnp.where(is_gate, log_f * LOG2_E, 0.0)
    tri = (lax.broadcasted_iota(jnp.int32, (rows, rows), 0)
           >= lax.broadcasted_iota(jnp.int32, (rows, rows), 1)).astype(BF16)
    cum = carry_ref[...]
    for piece in _split3(log_f):
        cum = cum + _dot(tri, piece.astype(BF16))
    carry_ref[...] = cum[rows - 1:rows, :]
    one = jnp.where(lane < 2 * N_SPLIT, 1.0, 0.0)
    for h in range(FOX_HEADS):
        col = jnp.sum(jnp.where(lane == FF_LANE + h, cum, 0.0), axis=1, keepdims=True)
        pieces = _split3(jnp.broadcast_to(col, (rows, LANES)))
        qa = jnp.where(lane < N_SPLIT, 0.0, one)
        ka = jnp.where(lane < N_SPLIT, one, 0.0)
        for i, piece in enumerate(pieces):
            qa = jnp.where(lane == i, piece, qa)
            ka = jnp.where(lane == N_SPLIT + i, -piece, ka)
        qa_ref[0, :, h * LANES:(h + 1) * LANES] = qa.astype(qa_ref.dtype)
        ka_ref[0, :, h * LANES:(h + 1) * LANES] = ka.astype(ka_ref.dtype)


def forget_bias_lanes(za, b_forget, *, rows=256):
    b, s, _ = za.shape
    rows = _tile(s, rows)
    bias = jnp.zeros((1, LANES), F32).at[0, FF_LANE:FF_LANE + FOX_HEADS].set(b_forget)
    aux = jax.ShapeDtypeStruct((b, s, FOX_HEADS * LANES), BF16)
    return pl.pallas_call(
        functools.partial(_forget_cumsum_kernel, rows=rows),
        grid=(b, s // rows),
        in_specs=[pl.BlockSpec((1, rows, LANES), lambda i, j: (i, j, A_KR // LANES)),
                  pl.BlockSpec((1, LANES), lambda i, j: (0, 0))],
        out_specs=[pl.BlockSpec((1, rows, FOX_HEADS * LANES), lambda i, j: (i, j, 0))] * 2,
        out_shape=[aux, aux],
        scratch_shapes=[pltpu.VMEM((1, LANES), F32)],
        compiler_params=_params("arbitrary", "arbitrary"),
        name="forget_bias_lanes",
    )(za, bias)


def _attn_kernel(qi_ref, ki_ref, *refs, has_aux, chunked_mask, hp, dk, dv, t):
    if has_aux:
        q_ref, qa_ref, k_ref, ka_ref, v_ref, o_ref, m_ref, acc_ref = refs
    else:
        q_ref, k_ref, v_ref, o_ref, m_ref, acc_ref = refs
    pair = pl.program_id(2)
    qi, ki = qi_ref[pair], ki_ref[pair]

    @pl.when(ki == 0)
    def _():
        m_ref[...] = jnp.full_like(m_ref, NEG_BIG)
        acc_ref[...] = jnp.zeros_like(acc_ref)

    ones = jnp.ones((t, LANES), BF16)

    def step(diagonal):
        if diagonal:
            row = lax.broadcasted_iota(jnp.int32, (t, t), 0)
            col = lax.broadcasted_iota(jnp.int32, (t, t), 1)
            keep = (col // CHUNK) <= (row // CHUNK) if chunked_mask else col <= row
        for h in range(hp):
            q = q_ref[0, :, h * dk:(h + 1) * dk]
            k = k_ref[0, :, h * dk:(h + 1) * dk]
            if has_aux:
                q = jnp.concatenate([q, qa_ref[0, :, h * LANES:(h + 1) * LANES]], axis=1)
                k = jnp.concatenate([k, ka_ref[0, :, h * LANES:(h + 1) * LANES]], axis=1)
            s = _dot_nt(q, k)
            if diagonal:
                s = jnp.where(keep, s, NEG_BIG)
            m_old = m_ref[h]
            m_new = jnp.maximum(m_old, jnp.max(s, axis=1, keepdims=True))
            alpha = jnp.exp2(m_old - m_new)
            p = jnp.exp2(s - jnp.concatenate([m_new] * (t // LANES), axis=1))
            v1 = jnp.concatenate([v_ref[0, :, h * dv:(h + 1) * dv], ones], axis=1)
            acc_ref[h] = jnp.concatenate([alpha, alpha], axis=1) * acc_ref[h] + _dot(p.astype(BF16), v1)
            m_ref[h] = m_new

    @pl.when(ki < qi)
    def _():
        step(False)

    @pl.when(ki == qi)
    def _():
        step(True)
        for h in range(hp):
            acc = acc_ref[h]
            o_ref[0, :, h * dv:(h + 1) * dv] = (acc[:, :dv] / acc[:, dv:]).astype(o_ref.dtype)


def attention(q_arr, q_col, k_arr, k_col, v_arr, v_col, *, heads, dk, dv, aux=None, chunked_mask,
              tile=512, hp=ATTN_HEADS_PER_STEP):
    assert dv == LANES and heads % hp == 0
    b, s, _ = q_arr.shape
    t = _tile(s, tile)
    n = s // t
    pairs = [(i, j) for i in range(n) for j in range(i + 1)]
    qi_tab = jnp.asarray([p[0] for p in pairs], jnp.int32)
    ki_tab = jnp.asarray([p[1] for p in pairs], jnp.int32)

    def spec(width, col, table):
        assert col % (hp * width) == 0
        first = col // (hp * width)
        if table == "q":
            return pl.BlockSpec((1, t, hp * width), lambda i, g, p, qt, kt: (i, qt[p], first + g))
        return pl.BlockSpec((1, t, hp * width), lambda i, g, p, qt, kt: (i, kt[p], first + g))

    if aux is None:
        in_specs = [spec(dk, q_col, "q"), spec(dk, k_col, "k"), spec(dv, v_col, "k")]
        args = [q_arr, k_arr, v_arr]
    else:
        in_specs = [spec(dk, q_col, "q"), spec(LANES, 0, "q"), spec(dk, k_col, "k"), spec(LANES, 0, "k"),
                    spec(dv, v_col, "k")]
        args = [q_arr, aux[0], k_arr, aux[1], v_arr]
    stat = pltpu.VMEM((hp, t, LANES), F32)
    return pl.pallas_call(
        functools.partial(_attn_kernel, has_aux=aux is not None, chunked_mask=chunked_mask, hp=hp, dk=dk,
                          dv=dv, t=t),
        grid_spec=pltpu.PrefetchScalarGridSpec(
            num_scalar_prefetch=2,
            grid=(b, heads // hp, len(pairs)),
            in_specs=in_specs,
            out_specs=spec(dv, 0, "q"),
            scratch_shapes=[stat, pltpu.VMEM((hp, t, dv + LANES), F32)]),
        out_shape=jax.ShapeDtypeStruct((b, s, heads * dv), BF16),
        compiler_params=_params("parallel", "parallel", "arbitrary"),
        name="fox_attention" if aux is not None else "mla_attention",
    )(qi_tab, ki_tab, *args)


def _mla_rope(x, mc, ma, mb):
    return x * mc + pltpu.roll(x, LANES - MLA_ROPE // 2, 1) * ma + pltpu.roll(x, MLA_ROPE // 2, 1) * mb


def _mla_q_kernel(c_ref, g_ref, w_ref, mc_ref, ma_ref, mb_ref, o_ref):
    x = c_ref[...]
    xn = (x * lax.rsqrt(jnp.mean(x * x, axis=-1, keepdims=True) + EPS) * g_ref[...]).astype(BF16)
    q = _dot(xn, w_ref[...])
    scale = (MLA_NOPE + MLA_ROPE) ** -0.5 * LOG2_E
    mc, ma, mb = mc_ref[...], ma_ref[...], mb_ref[...]
    for h in range(MLA_HEADS):
        lo = h * MLA_QK_PAD
        o_ref[:, lo:lo + MLA_NOPE] = (q[:, lo:lo + MLA_NOPE] * scale).astype(o_ref.dtype)
        pe = _mla_rope(q[:, lo + MLA_NOPE:lo + MLA_QK_PAD], mc, ma, mb)
        o_ref[:, lo + MLA_NOPE:lo + MLA_QK_PAD] = (pe * scale).astype(o_ref.dtype)


def mla_queries(za2, q_norm, w_uq_pad, layer, mc, ma, mb, *, bm=512):
    t = za2.shape[0]
    bm = _tile(t, bm, SUBLANES)
    tab = pl.BlockSpec((bm, LANES), lambda i: (i, 0))
    return pl.pallas_call(
        _mla_q_kernel,
        grid=(t // bm,),
        in_specs=[pl.BlockSpec((bm, MLA_Q_LORA), lambda i: (i, A_CQ // MLA_Q_LORA)),
                  pl.BlockSpec((1, MLA_Q_LORA), lambda i: (0, 0)),
                  pl.BlockSpec((None,) + w_uq_pad.shape[1:], lambda i: (layer, 0, 0)),
                  tab, tab, tab],
        out_specs=pl.BlockSpec((bm, MLA_HEADS * MLA_QK_PAD), lambda i: (i, 0)),
        out_shape=jax.ShapeDtypeStruct((t, MLA_HEADS * MLA_QK_PAD), BF16),
        compiler_params=_params("parallel"),
        name="mla_queries",
    )(za2, q_norm.reshape(1, -1), w_uq_pad, mc, ma, mb)


def _mla_kv_kernel(c_ref, kr_ref, g_ref, wk_ref, wv_ref, mc_ref, ma_ref, mb_ref, k_ref, v_ref):
    x = c_ref[...]
    xn = (x * lax.rsqrt(jnp.mean(x * x, axis=-1, keepdims=True) + EPS) * g_ref[...]).astype(BF16)
    k_nope = _dot(xn, wk_ref[...])
    v_ref[...] = _dot(xn, wv_ref[...]).astype(v_ref.dtype)
    k_pe = _mla_rope(kr_ref[...], mc_ref[...], ma_ref[...], mb_ref[...]).astype(k_ref.dtype)
    for h in range(MLA_HEADS):
        lo = h * MLA_QK_PAD
        k_ref[:, lo:lo + MLA_NOPE] = k_nope[:, h * MLA_NOPE:(h + 1) * MLA_NOPE].astype(k_ref.dtype)
        k_ref[:, lo + MLA_NOPE:lo + MLA_QK_PAD] = k_pe


def mla_keys_values(za2, kv_norm, wk, wv, layer, mc, ma, mb, *, bm=512):
    t = za2.shape[0]
    bm = _tile(t, bm, SUBLANES)
    tab = pl.BlockSpec((bm, LANES), lambda i: (i, 0))
    w_spec = pl.BlockSpec((None,) + wk.shape[1:], lambda i: (layer, 0, 0))
    return pl.pallas_call(
        _mla_kv_kernel,
        grid=(t // bm,),
        in_specs=[pl.BlockSpec((bm, MLA_KV_LORA), lambda i: (i, A_CKV // MLA_KV_LORA)),
                  pl.BlockSpec((bm, LANES), lambda i: (i, A_KR // LANES)),
                  pl.BlockSpec((1, MLA_KV_LORA), lambda i: (0, 0)),
                  w_spec, w_spec, tab, tab, tab],
        out_specs=[pl.BlockSpec((bm, MLA_HEADS * MLA_QK_PAD), lambda i: (i, 0)),
                   pl.BlockSpec((bm, MLA_HEADS * MLA_DV), lambda i: (i, 0))],
        out_shape=[jax.ShapeDtypeStruct((t, MLA_HEADS * MLA_QK_PAD), BF16),
                   jax.ShapeDtypeStruct((t, MLA_HEADS * MLA_DV), BF16)],
        compiler_params=_params("parallel"),
        name="mla_keys_values",
    )(za2, za2, kv_norm.reshape(1, -1), wk, wv, mc, ma, mb)


def _prepare_weights(p):
    w_in_t = jnp.swapaxes(p["w_in"], 1, 2)
    depth, width, d = w_in_t.shape
    rq_rk_end = 2 * RET_HEADS * RET_DK
    bf_end = rq_rk_end + B_WIDTH
    ff_end = bf_end + FOX_HEADS
    fq0 = rq_rk_end + B_FQ
    row_scale = jnp.ones((width, 1), F32).at[fq0:fq0 + FOX_HEADS * FOX_DH].set(FOX_DH ** -0.5 * LOG2_E)
    w_in_b = (w_in_t * row_scale).astype(BF16)
    used = rq_rk_end + (width - ff_end) + FOX_HEADS
    w_in_a = jnp.concatenate([w_in_b[:, :rq_rk_end], w_in_b[:, ff_end:], w_in_b[:, bf_end:ff_end],
                              jnp.zeros((depth, A_WIDTH - used, d), BF16)], axis=1)
    w_uq = p["w_uq"].reshape(depth, MLA_Q_LORA, MLA_HEADS, MLA_NOPE + MLA_ROPE)
    w_uq = jnp.pad(w_uq, ((0, 0), (0, 0), (0, 0), (0, MLA_QK_PAD - MLA_NOPE - MLA_ROPE)))
    w_ukv = p["w_ukv"].reshape(depth, MLA_KV_LORA, MLA_HEADS, MLA_NOPE + MLA_DV)
    out = dict(
        w13_first=p["ffn1_w13"][:1].astype(BF16),
        w_in_a=w_in_a, w_in_b=w_in_b, w_in_b_col=rq_rk_end,
        w_uq=w_uq.reshape(depth, MLA_Q_LORA, MLA_HEADS * MLA_QK_PAD).astype(BF16),
        w_uk=w_ukv[..., :MLA_NOPE].reshape(depth, MLA_KV_LORA, MLA_HEADS * MLA_NOPE).astype(BF16),
        w_uv=w_ukv[..., MLA_NOPE:].reshape(depth, MLA_KV_LORA, MLA_HEADS * MLA_DV).astype(BF16))
    return out


def _ffn(h, u, ssq, w13, w13_layer, w2_f32, layer, next_gain, up_casts):
    g, (w2, *up_done) = swiglu_up(u, ssq, w13, w13_layer, casts=[(w2_f32, layer), *up_casts])
    result, _ = residual_matmul(g, w2[None], 0, h, scale=0.5, gain=next_gain, bm=512, bn=512, name="ffn_down")
    return result, up_done


def _mixer(h, u, ssq, batch, tables, p, w, w_gate, layer, next_gain, proj_casts, merge_casts):
    rc, rs, mc, ma, mb = tables
    t, d = h.shape
    s = t // batch
    za, (w_up_ret, w_up_fox, w_up_mla, *proj_done) = normed_matmul(
        u, ssq, w["w_in_a"], layer, out_dtype=F32, name="in_proj_f32",
        casts=[(p["w_up_ret"], layer), (p["w_up_fox"], layer), (p["w_up_mla"], layer), *proj_casts])
    zb, _ = normed_matmul(u, ssq, w["w_in_b"], layer, col=w["w_in_b_col"], n=B_WIDTH, out_dtype=BF16,
                          name="in_proj_bf16")
    za3 = za.reshape(batch, s, A_WIDTH)
    zb3 = zb.reshape(batch, s, B_WIDTH)
    tab3 = lambda x: x.reshape(batch, s, LANES)

    ro = retention(za3, zb3, tab3(rc), tab3(rs), p["ret_norm"][layer])
    fo = attention(zb3, B_FQ, zb3, B_FK, zb3, B_FV, heads=FOX_HEADS, dk=FOX_DH, dv=FOX_DH,
                   aux=forget_bias_lanes(za3, p["b_forget"][layer]), chunked_mask=False)
    qm = mla_queries(za, p["mla_q_norm"][layer], w["w_uq"], layer, mc, ma, mb)
    km, vm = mla_keys_values(za, p["mla_kv_norm"][layer], w["w_uk"], w["w_uv"], layer, mc, ma, mb)
    mo = attention(qm.reshape(batch, s, -1), 0, km.reshape(batch, s, -1), 0, vm.reshape(batch, s, -1), 0,
                   heads=MLA_HEADS, dk=MLA_QK_PAD, dv=MLA_DV, chunked_mask=True)

    merged, (w_out, *merge_done) = gated_merge(
        u, ssq, ro.reshape(t, -1), fo.reshape(t, -1), mo.reshape(t, -1), w_gate, p["b_gate"], layer,
        w_up_ret, w_up_fox, w_up_mla, casts=[(p["w_out"], layer), *merge_casts])
    result, _ = residual_matmul(merged, w_out[None], 0, h, scale=1.0, gain=next_gain, name="out_proj")
    return result, proj_done + merge_done


def kernel(x, positions, ffn1_norm, ffn1_w13, ffn1_w2, mix_norm, w_in, b_forget, ret_norm, mla_q_norm,
           mla_kv_norm, w_uq, w_ukv, w_up_ret, w_up_fox, w_up_mla, w_gate, b_gate, w_out, ffn2_norm,
           ffn2_w13, ffn2_w2, final_norm):
    p = dict(mix_norm=mix_norm, w_in=w_in, b_forget=b_forget, ret_norm=ret_norm, mla_q_norm=mla_q_norm,
             mla_kv_norm=mla_kv_norm, w_uq=w_uq, w_ukv=w_ukv, w_up_ret=w_up_ret, w_up_fox=w_up_fox,
             w_up_mla=w_up_mla, w_gate=w_gate, b_gate=b_gate, w_out=w_out, ffn1_w13=ffn1_w13,
             ffn1_w2=ffn1_w2, ffn2_w13=ffn2_w13, ffn2_w2=ffn2_w2)
    batch, s, d = x.shape
    h = x.reshape(batch * s, d)
    w = _prepare_weights(p)
    tables = rope_tables(positions)
    depth = ffn1_norm.shape[0]
    u, ssq = prenorm(h, ffn1_norm[0])
    w_gate_rows = w_gate.reshape(depth, 3 * d, d)
    w13, w13_layer = w["w13_first"], 0
    for layer in range(depth):
        last = layer + 1 == depth
        (h, u, ssq), (gate,) = _ffn(h, u, ssq, w13, w13_layer, ffn1_w2, layer, mix_norm[layer],
                                    [(w_gate_rows, layer)])
        (h, u, ssq), (w2_b, w13_b) = _mixer(h, u, ssq, batch, tables, p, w, gate.reshape(3, d, d), layer,
                                            ffn2_norm[layer], [(ffn2_w2, layer)], [(ffn2_w13, layer)])
        g, next_w13 = swiglu_up(u, ssq, w13_b[None], 0, casts=[] if last else [(ffn1_w13, layer + 1)])
        result, _ = residual_matmul(g, w2_b[None], 0, h, scale=0.5, gain=None if last else ffn1_norm[layer + 1],
                                    bm=512, bn=512, name="ffn_down")
        if last:
            h = result
        else:
            h, u, ssq = result
            w13, w13_layer = next_w13[0][None], 0
    return rmsnorm(h, final_norm, x.dtype).reshape(batch, s, d)
```

```python
import functools
import math

import jax
import jax.numpy as jnp
from jax import lax
from jax.experimental import pallas as pl
from jax.experimental.pallas import tpu as pltpu

F32 = jnp.float32
BF16 = jnp.bfloat16

CHUNK = 64
RET_HEADS, RET_DK, RET_DV = 8, 128, 256
FOX_HEADS, FOX_DH = 8, 128
MLA_HEADS, MLA_Q_LORA, MLA_KV_LORA = 8, 1024, 512
MLA_NOPE, MLA_ROPE, MLA_DV = 128, 64, 128
MLA_QK_PAD = 256
ROPE_BASE = 10000.0
EPS = 1e-6
NEG_BIG = -1e30
LOG2_E = math.log2(math.e)
N_SPLIT = 3
ATTN_HEADS_PER_STEP = 8

LANES = 128
SUBLANES = 8
BF16_ROWS = 2 * SUBLANES
VMEM_LIMIT_BYTES = 62 * 1024 * 1024

A_RQ, A_RK, A_CQ, A_CKV, A_KR = 0, 1024, 2048, 3072, 3584
A_FF = A_KR + MLA_ROPE
A_WIDTH = 3840
A_TILE = 1280
FF_LANE = A_FF - A_KR
B_RV, B_RG, B_FQ, B_FK, B_FV = 0, 2048, 4096, 5120, 6144
B_WIDTH = 7168


def _tile(dim, pref, quantum=LANES):
    if dim <= pref:
        return dim
    t = (pref // quantum) * quantum
    while t > quantum and dim % t:
        t -= quantum
    assert dim % t == 0, (dim, pref)
    return t


def _params(*sem):
    return pltpu.CompilerParams(dimension_semantics=sem, vmem_limit_bytes=VMEM_LIMIT_BYTES)


def _sigmoid(x):
    return 0.5 + 0.5 * jnp.tanh(0.5 * x)


def _dot(a, b):
    return jnp.dot(a, b, preferred_element_type=F32)


def _dot_nt(a, b):
    return lax.dot_general(a, b, (((1,), (1,)), ((), ())), preferred_element_type=F32)


def _dot_tn(a, b):
    return lax.dot_general(a, b, (((0,), (0,)), ((), ())), preferred_element_type=F32)


def _cast_fields(cast):
    stack, layer, rows, row_scale = (*cast, None, None)[:4]
    return stack, layer, stack.shape[1] if rows is None else rows, row_scale


def _cast_plan(casts, grid):
    steps = math.prod(grid)

    def linear(ids):
        idx = ids[0]
        for extent, i in zip(grid[1:], ids[1:]):
            idx = idx * extent + i
        return idx

    src_specs, scale_specs, out_specs, out_shapes, sources, scales = [], [], [], [], [], []
    for cast in casts:
        stack, layer, rows, row_scale = _cast_fields(cast)
        cols = stack.shape[2]
        blocks = max(n for n in range(1, steps + 1) if rows % n == 0 and (rows // n) % BF16_ROWS == 0)
        rb, rep = rows // blocks, steps // blocks
        block_of = lambda ids, rep=rep, last=blocks - 1: jnp.minimum(linear(ids) // rep, last)
        src_specs.append(pl.BlockSpec((None, rb, cols),
                                      lambda *ids, layer=layer, block_of=block_of: (layer, block_of(ids), 0)))
        out_specs.append(pl.BlockSpec((rb, cols), lambda *ids, block_of=block_of: (block_of(ids), 0)))
        out_shapes.append(jax.ShapeDtypeStruct((rows, cols), BF16))
        sources.append(stack)
        if row_scale is not None:
            scale_specs.append(pl.BlockSpec((rb, 1), lambda *ids, block_of=block_of: (block_of(ids), 0)))
            scales.append(row_scale)
    return src_specs + scale_specs, out_specs, out_shapes, sources + scales


def _hosting(body, n_in, n_out, casts):
    if not casts:
        return body
    scaled = [_cast_fields(c)[3] is not None for c in casts]
    n_cast, n_scale = len(casts), sum(scaled)

    def kernel(*refs):
        ins, rest = refs[:n_in], refs[n_in:]
        srcs, rest = rest[:n_cast], rest[n_cast:]
        scale_refs, rest = list(rest[:n_scale]), rest[n_scale:]
        outs, rest = rest[:n_out], rest[n_out:]
        dsts, scratch = rest[:n_cast], rest[n_cast:]
        body(*ins, *outs, *scratch)
        for src, dst, has_scale in zip(srcs, dsts, scaled):
            x = src[...]
            if has_scale:
                x = x * scale_refs.pop(0)[...]
            dst[...] = x.astype(dst.dtype)

    return kernel


def _rmsnorm_kernel(x_ref, g_ref, o_ref):
    x = x_ref[...]
    ms = jnp.mean(x * x, axis=-1, keepdims=True)
    o_ref[...] = (x * lax.rsqrt(ms + EPS) * g_ref[...]).astype(o_ref.dtype)


def rmsnorm(x, g, out_dtype):
    m, d = x.shape
    bm = _tile(m, 512, SUBLANES)
    return pl.pallas_call(
        _rmsnorm_kernel,
        grid=(m // bm,),
        in_specs=[pl.BlockSpec((bm, d), lambda i: (i, 0)),
                  pl.BlockSpec((1, d), lambda i: (0, 0))],
        out_specs=pl.BlockSpec((bm, d), lambda i: (i, 0)),
        out_shape=jax.ShapeDtypeStruct((m, d), out_dtype),
        compiler_params=_params("parallel"),
        name="rmsnorm",
    )(x, g.reshape(1, d))


def _lane_partial_ssq(x):
    sq = x * x
    part = sq[:, :LANES]
    for c in range(1, x.shape[1] // LANES):
        part = part + sq[:, c * LANES:(c + 1) * LANES]
    return part


def _row_rstd(ssq_ref, inv_d):
    return lax.rsqrt(jnp.sum(ssq_ref[...], axis=1, keepdims=True) * inv_d + EPS)


def _prenorm_kernel(x_ref, g_ref, u_ref, ssq_ref):
    x = x_ref[...]
    u_ref[...] = (x * g_ref[...]).astype(u_ref.dtype)
    ssq_ref[...] = _lane_partial_ssq(x)


def prenorm(x, g):
    m, d = x.shape
    bm = _tile(m, 512, SUBLANES)
    return pl.pallas_call(
        _prenorm_kernel,
        grid=(m // bm,),
        in_specs=[pl.BlockSpec((bm, d), lambda i: (i, 0)),
                  pl.BlockSpec((1, d), lambda i: (0, 0))],
        out_specs=[pl.BlockSpec((bm, d), lambda i: (i, 0)),
                   pl.BlockSpec((bm, LANES), lambda i: (i, 0))],
        out_shape=[jax.ShapeDtypeStruct((m, d), BF16), jax.ShapeDtypeStruct((m, LANES), F32)],
        compiler_params=_params("parallel"),
        name="prenorm",
    )(x, g.reshape(1, d))


def _proj_kernel(a_ref, ssq_ref, wt_ref, o_ref, *, inv_d):
    acc = _dot_nt(a_ref[...], wt_ref[...])
    o_ref[...] = (acc * _row_rstd(ssq_ref, inv_d)).astype(o_ref.dtype)


def normed_matmul(a, ssq, wt, layer, *, col=0, n=None, out_dtype, casts=(), bm=1024, bn=1024, name):
    m, kd = a.shape
    n = wt.shape[1] if n is None else n
    bm, bn = _tile(m, bm, SUBLANES), _tile(math.gcd(n, col) if col else n, bn)
    first = col // bn
    grid = (m // bm, n // bn)
    c_in, c_out, c_shape, c_args = _cast_plan(casts, grid)
    out, *casted = pl.pallas_call(
        _hosting(functools.partial(_proj_kernel, inv_d=1.0 / kd), 3, 1, casts),
        grid=grid,
        in_specs=[pl.BlockSpec((bm, kd), lambda i, j: (i, 0)),
                  pl.BlockSpec((bm, LANES), lambda i, j: (i, 0)),
                  pl.BlockSpec((None, bn, kd), lambda i, j: (layer, first + j, 0))] + c_in,
        out_specs=[pl.BlockSpec((bm, bn), lambda i, j: (i, j))] + c_out,
        out_shape=[jax.ShapeDtypeStruct((m, n), out_dtype)] + c_shape,
        compiler_params=_params("arbitrary", "arbitrary"),
        name=name,
    )(a, ssq, wt, *c_args)
    return out, casted


def _residual_kernel(a_ref, w_ref, r_ref, *refs, nk, scale, emit_norm):
    if emit_norm:
        g_ref, o_ref, u_ref, ssq_ref = refs
    else:
        o_ref, = refs

    def finish(h):
        o_ref[...] = h
        if emit_norm:
            u_ref[...] = (h * g_ref[...]).astype(u_ref.dtype)
            row_ssq = _lane_partial_ssq(h)
            j = pl.program_id(1)

            @pl.when(j == 0)
            def _():
                ssq_ref[...] = row_ssq

            @pl.when(j > 0)
            def _():
                ssq_ref[...] += row_ssq

    part = scale * _dot(a_ref[...], w_ref[...])
    if nk == 1:
        finish(r_ref[...] + part)
        return
    k = pl.program_id(2)

    @pl.when(k == 0)
    def _():
        o_ref[...] = r_ref[...] + part

    @pl.when(jnp.logical_and(k > 0, k < nk - 1))
    def _():
        o_ref[...] += part

    @pl.when(k == nk - 1)
    def _():
        finish(o_ref[...] + part)


def residual_matmul(a, w, layer, res, *, scale, gain=None, casts=(), bm=1024, bn=1024, bk=None, name):
    m, kd = a.shape
    n = w.shape[2]
    bm, bn = _tile(m, bm, SUBLANES), _tile(n, bn)
    bk = kd if bk is None else _tile(kd, bk)
    nk = kd // bk
    tile = pl.BlockSpec((bm, bn), lambda i, j, k: (i, j))
    in_specs = [pl.BlockSpec((bm, bk), lambda i, j, k: (i, k)),
                pl.BlockSpec((None, bk, bn), lambda i, j, k: (layer, k, j)),
                tile]
    args = [a, w, res]
    out_specs, out_shape = [tile], [jax.ShapeDtypeStruct((m, n), F32)]
    if gain is not None:
        in_specs.append(pl.BlockSpec((1, bn), lambda i, j, k: (0, j)))
        args.append(gain.reshape(1, n))
        out_specs += [tile, pl.BlockSpec((bm, LANES), lambda i, j, k: (i, 0))]
        out_shape += [jax.ShapeDtypeStruct((m, n), BF16), jax.ShapeDtypeStruct((m, LANES), F32)]
    n_out = len(out_specs)
    grid = (m // bm, n // bn, nk)
    c_in, c_out, c_shape, c_args = _cast_plan(casts, grid)
    outs = pl.pallas_call(
        _hosting(functools.partial(_residual_kernel, nk=nk, scale=scale, emit_norm=gain is not None),
                 len(args), n_out, casts),
        grid=grid,
        in_specs=in_specs + c_in,
        out_specs=out_specs + c_out,
        out_shape=out_shape + c_shape,
        compiler_params=_params("parallel", "arbitrary", "arbitrary"),
        name=name,
    )(*args, *c_args)
    result = outs[0] if gain is None else tuple(outs[:n_out])
    return result, list(outs[n_out:])


def _swiglu_kernel(u_ref, ssq_ref, w1_ref, w3_ref, o_ref, *, inv_d):
    u = u_ref[...]
    rstd = _row_rstd(ssq_ref, inv_d)
    a = _dot(u, w1_ref[...]) * rstd
    b = _dot(u, w3_ref[...]) * rstd
    o_ref[...] = (a * _sigmoid(a) * b).astype(o_ref.dtype)


def swiglu_up(u, ssq, w13, layer, *, casts=(), bm=1024, bn=512):
    m, d = u.shape
    f = w13.shape[2] // 2
    bm, bn = _tile(m, bm, SUBLANES), _tile(f, bn)
    nf = f // bn
    grid = (m // bm, nf)
    c_in, c_out, c_shape, c_args = _cast_plan(casts, grid)
    out, *casted = pl.pallas_call(
        _hosting(functools.partial(_swiglu_kernel, inv_d=1.0 / d), 4, 1, casts),
        grid=grid,
        in_specs=[pl.BlockSpec((bm, d), lambda i, j: (i, 0)),
                  pl.BlockSpec((bm, LANES), lambda i, j: (i, 0)),
                  pl.BlockSpec((None, d, bn), lambda i, j: (layer, 0, j)),
                  pl.BlockSpec((None, d, bn), lambda i, j: (layer, 0, j + nf))] + c_in,
        out_specs=[pl.BlockSpec((bm, bn), lambda i, j: (i, j))] + c_out,
        out_shape=[jax.ShapeDtypeStruct((m, f), BF16)] + c_shape,
        compiler_params=_params("arbitrary", "arbitrary"),
        name="swiglu_up",
    )(u, ssq, w13, w13, *c_args)
    return out, casted


def _merge_kernel(u_ref, ssq_ref, ro_ref, fo_ref, mo_ref, wg_ref, bg_ref, wr_ref, wf_ref, wm_ref, o_ref, *,
                  inv_d):
    u = u_ref[...]
    rstd = _row_rstd(ssq_ref, inv_d)
    total = None
    for i, (x_ref, w_ref) in enumerate(((ro_ref, wr_ref), (fo_ref, wf_ref), (mo_ref, wm_ref))):
        gate = jax.nn.sigmoid(_dot(u, wg_ref[i]) * rstd + bg_ref[i:i + 1, :])
        term = gate * _dot(x_ref[...], w_ref[...])
        total = term if total is None else total + term
    o_ref[...] = total.astype(o_ref.dtype)


def gated_merge(u, ssq, ro, fo, mo, w_gate, b_gate, layer, w_up_ret, w_up_fox, w_up_mla, *, casts=(),
                bm=512, bn=512):
    m, d = u.shape
    n = w_gate.shape[-1]
    bm, bn = _tile(m, bm, SUBLANES), _tile(n, bn)
    row = lambda width: pl.BlockSpec((bm, width), lambda i, j: (i, 0))
    col = lambda depth: pl.BlockSpec((depth, bn), lambda i, j: (0, j))
    grid = (m // bm, n // bn)
    c_in, c_out, c_shape, c_args = _cast_plan(casts, grid)
    out, *casted = pl.pallas_call(
        _hosting(functools.partial(_merge_kernel, inv_d=1.0 / d), 10, 1, casts),
        grid=grid,
        in_specs=[row(d), row(LANES), row(ro.shape[1]), row(fo.shape[1]), row(mo.shape[1]),
                  pl.BlockSpec((3, d, bn), lambda i, j: (0, 0, j)),
                  pl.BlockSpec((None, 3, bn), lambda i, j: (layer, 0, j)),
                  col(ro.shape[1]), col(fo.shape[1]), col(mo.shape[1])] + c_in,
        out_specs=[pl.BlockSpec((bm, bn), lambda i, j: (i, j))] + c_out,
        out_shape=[jax.ShapeDtypeStruct((m, n), BF16)] + c_shape,
        compiler_params=_params("arbitrary", "arbitrary"),
        name="gated_merge",
    )(u, ssq, ro, fo, mo, w_gate, b_gate, w_up_ret, w_up_fox, w_up_mla, *c_args)
    return out, casted


def _rope_table_kernel(pos_ref, inv_ref, coef_ref, rc_ref, rs_ref, mc_ref, ma_ref, mb_ref):
    p = pos_ref[...].astype(F32)
    ang_r = p * inv_ref[0:1, :]
    ang_m = p * inv_ref[1:2, :]
    rc_ref[...] = jnp.cos(ang_r)
    rs_ref[...] = jnp.sin(ang_r) * coef_ref[0:1, :]
    sin_m = jnp.sin(ang_m)
    mc_ref[...] = jnp.cos(ang_m) * coef_ref[1:2, :]
    ma_ref[...] = sin_m * coef_ref[2:3, :]
    mb_ref[...] = sin_m * coef_ref[3:4, :]


def rope_tables(positions):
    t = positions.size
    h_r, h_m = RET_DK // 2, MLA_ROPE // 2
    inv_r = ROPE_BASE ** (-jnp.arange(h_r, dtype=F32) / h_r)
    inv_m = ROPE_BASE ** (-jnp.arange(h_m, dtype=F32) / h_m)
    zeros_m = jnp.zeros((LANES - 2 * h_m,), F32)
    ones_m = jnp.ones((h_m,), F32)
    inv = jnp.stack([jnp.concatenate([inv_r, inv_r]),
                     jnp.concatenate([inv_m, inv_m, zeros_m])])
    coef = jnp.stack([
        jnp.concatenate([-jnp.ones((h_r,), F32), jnp.ones((h_r,), F32)]),
        jnp.concatenate([ones_m, ones_m, zeros_m]),
        jnp.concatenate([-ones_m, 0 * ones_m, zeros_m]),
        jnp.concatenate([0 * ones_m, ones_m, zeros_m]),
    ])
    bm = _tile(t, 1024, SUBLANES)
    tab = jax.ShapeDtypeStruct((t, LANES), F32)
    return pl.pallas_call(
        _rope_table_kernel,
        grid=(t // bm,),
        in_specs=[pl.BlockSpec((bm, 1), lambda i: (i, 0)),
                  pl.BlockSpec((2, LANES), lambda i: (0, 0)),
                  pl.BlockSpec((4, LANES), lambda i: (0, 0))],
        out_specs=[pl.BlockSpec((bm, LANES), lambda i: (i, 0))] * 5,
        out_shape=[tab] * 5,
        compiler_params=_params("parallel"),
        name="rope_tables",
    )(positions.reshape(t, 1), inv, coef)


def _ret_log_gamma(h):
    return math.log1p(-2.0 ** (-5.0 - h))


def _retention_kernel(q_ref, k_ref, v_ref, g_ref, rc_ref, rs_ref, norm_ref, o_ref, state_ref, decay_ref,
                      *, blk):
    first = jnp.logical_and(pl.program_id(0) == 0, pl.program_id(1) == 0)

    @pl.when(first)
    def _():
        row = lax.broadcasted_iota(jnp.int32, (blk, blk), 0)
        col = lax.broadcasted_iota(jnp.int32, (blk, blk), 1)
        dist = jnp.abs(row - col).astype(F32)
        visible = (col // CHUNK) <= (row // CHUNK)
        for h in range(RET_HEADS):
            decay_ref[h] = jnp.where(visible, jnp.exp(_ret_log_gamma(h) * dist), 0.0)

    @pl.when(pl.program_id(1) == 0)
    def _():
        state_ref[...] = jnp.zeros_like(state_ref)

    rc = rc_ref[0]
    rs = rs_ref[0]
    idx = lax.broadcasted_iota(jnp.int32, (blk, 1), 0).astype(F32)
    for h in range(RET_HEADS):
        lg = _ret_log_gamma(h)
        qk = slice(h * RET_DK, (h + 1) * RET_DK)
        vv = slice(h * RET_DV, (h + 1) * RET_DV)
        q = q_ref[0, :, qk]
        k = k_ref[0, :, qk]
        q = (q * rc + pltpu.roll(q, RET_DK // 2, 1) * rs) * (RET_DK ** -0.5)
        k = k * rc + pltpu.roll(k, RET_DK // 2, 1) * rs
        v = v_ref[0, :, vv]
        state = state_ref[h]
        scores = _dot_nt(q.astype(BF16), k.astype(BF16)) * decay_ref[h]
        q_in = (q * jnp.exp(lg * (idx + 1.0))).astype(BF16)
        out = _dot(scores.astype(BF16), v) + _dot(q_in, state.astype(BF16))
        k_out = (k * jnp.exp(lg * (blk - 1.0 - idx))).astype(BF16)
        state_ref[h] = state * math.exp(lg * blk) + _dot_tn(k_out, v)
        out = out * lax.rsqrt(jnp.mean(out * out, axis=-1, keepdims=True) + EPS)
        gate = g_ref[0, :, vv].astype(F32)
        out = out * norm_ref[:, vv] * (gate * _sigmoid(gate))
        o_ref[0, :, vv] = out.astype(o_ref.dtype)


def retention(za, zb, rc, rs, ret_norm, *, blk=256):
    b, s, _ = za.shape
    blk = _tile(s, blk, CHUNK)
    qw, vw = RET_HEADS * RET_DK, RET_HEADS * RET_DV
    return pl.pallas_call(
        functools.partial(_retention_kernel, blk=blk),
        grid=(b, s // blk),
        in_specs=[pl.BlockSpec((1, blk, qw), lambda i, j: (i, j, A_RQ // qw)),
                  pl.BlockSpec((1, blk, qw), lambda i, j: (i, j, A_RK // qw)),
                  pl.BlockSpec((1, blk, vw), lambda i, j: (i, j, B_RV // vw)),
                  pl.BlockSpec((1, blk, vw), lambda i, j: (i, j, B_RG // vw)),
                  pl.BlockSpec((1, blk, LANES), lambda i, j: (i, j, 0)),
                  pl.BlockSpec((1, blk, LANES), lambda i, j: (i, j, 0)),
                  pl.BlockSpec((1, vw), lambda i, j: (0, 0))],
        out_specs=pl.BlockSpec((1, blk, vw), lambda i, j: (i, j, 0)),
        out_shape=jax.ShapeDtypeStruct((b, s, vw), BF16),
        scratch_shapes=[pltpu.VMEM((RET_HEADS, RET_DK, RET_DV), F32),
                        pltpu.VMEM((RET_HEADS, blk, blk), F32)],
        compiler_params=_params("arbitrary", "arbitrary"),
        name="retention",
    )(za, za, zb, zb, rc, rs, ret_norm.reshape(1, vw))


def _split3(x):
    hi = x.astype(BF16).astype(F32)
    rest = x - hi
    mid = rest.astype(BF16).astype(F32)
    lo = (rest - mid).astype(BF16).astype(F32)
    return hi, mid, lo


def _forget_cumsum_kernel(z_ref, bias_ref, qa_ref, ka_ref, carry_ref, *, rows):
    @pl.when(pl.program_id(1) == 0)
    def _():
        carry_ref[...] = jnp.zeros_like(carry_ref)

    lane = lax.broadcasted_iota(jnp.int32, (rows, LANES), 1)
    is_gate = jnp.logical_and(lane >= FF_LANE, lane < FF_LANE + FOX_HEADS)
    x = z_ref[0] + bias_ref[...]
    log_f = jnp.minimum(x, 0.0) - jnp.log(1.0 + jnp.exp(-jnp.abs(x)))
    log_f = jnp.where(is_gate, log_f * LOG2_E, 0.0)
    tri = (lax.broadcasted_iota(jnp.int32, (rows, rows), 0)
           >= lax.broadcasted_iota(jnp.int32, (rows, rows), 1)).astype(BF16)
    cum = carry_ref[...]
    for piece in _split3(log_f):
        cum = cum + _dot(tri, piece.astype(BF16))
    carry_ref[...] = cum[rows - 1:rows, :]
    one = jnp.where(lane < 2 * N_SPLIT, 1.0, 0.0)
    for h in range(FOX_HEADS):
        col = jnp.sum(jnp.where(lane == FF_LANE + h, cum, 0.0), axis=1, keepdims=True)
        pieces = _split3(jnp.broadcast_to(col, (rows, LANES)))
        qa = jnp.where(lane < N_SPLIT, 0.0, one)
        ka = jnp.where(lane < N_SPLIT, one, 0.0)
        for i, piece in enumerate(pieces):
            qa = jnp.where(lane == i, piece, qa)
            ka = jnp.where(lane == N_SPLIT + i, -piece, ka)
        qa_ref[0, :, h * LANES:(h + 1) * LANES] = qa.astype(qa_ref.dtype)
        ka_ref[0, :, h * LANES:(h + 1) * LANES] = ka.astype(ka_ref.dtype)


def forget_bias_lanes(za, b_forget, *, rows=256):
    b, s, _ = za.shape
    rows = _tile(s, rows)
    bias = jnp.zeros((1, LANES), F32).at[0, FF_LANE:FF_LANE + FOX_HEADS].set(b_forget)
    aux = jax.ShapeDtypeStruct((b, s, FOX_HEADS * LANES), BF16)
    return pl.pallas_call(
        functools.partial(_forget_cumsum_kernel, rows=rows),
        grid=(b, s // rows),
        in_specs=[pl.BlockSpec((1, rows, LANES), lambda i, j: (i, j, A_KR // LANES)),
                  pl.BlockSpec((1, LANES), lambda i, j: (0, 0))],
        out_specs=[pl.BlockSpec((1, rows, FOX_HEADS * LANES), lambda i, j: (i, j, 0))] * 2,
        out_shape=[aux, aux],
        scratch_shapes=[pltpu.VMEM((1, LANES), F32)],
        compiler_params=_params("arbitrary", "arbitrary"),
        name="forget_bias_lanes",
    )(za, bias)


def _attn_kernel(qi_ref, ki_ref, *refs, has_aux, chunked_mask, hp, dk, dv, t):
    if has_aux:
        q_ref, qa_ref, k_ref, ka_ref, v_ref, o_ref, m_ref, acc_ref = refs
    else:
        q_ref, k_ref, v_ref, o_ref, m_ref, acc_ref = refs
    pair = pl.program_id(2)
    qi, ki = qi_ref[pair], ki_ref[pair]

    @pl.when(ki == 0)
    def _():
        m_ref[...] = jnp.full_like(m_ref, NEG_BIG)
        acc_ref[...] = jnp.zeros_like(acc_ref)

    ones = jnp.ones((t, LANES), BF16)

    def step(diagonal):
        if diagonal:
            row = lax.broadcasted_iota(jnp.int32, (t, t), 0)
            col = lax.broadcasted_iota(jnp.int32, (t, t), 1)
            keep = (col // CHUNK) <= (row // CHUNK) if chunked_mask else col <= row
        for h in range(hp):
            q = q_ref[0, :, h * dk:(h + 1) * dk]
            k = k_ref[0, :, h * dk:(h + 1) * dk]
            if has_aux:
                q = jnp.concatenate([q, qa_ref[0, :, h * LANES:(h + 1) * LANES]], axis=1)
                k = jnp.concatenate([k, ka_ref[0, :, h * LANES:(h + 1) * LANES]], axis=1)
            s = _dot_nt(q, k)
            if diagonal:
                s = jnp.where(keep, s, NEG_BIG)
            m_old = m_ref[h]
            m_new = jnp.maximum(m_old, jnp.max(s, axis=1, keepdims=True))
            alpha = jnp.exp2(m_old - m_new)
            p = jnp.exp2(s - jnp.concatenate([m_new] * (t // LANES), axis=1))
            v1 = jnp.concatenate([v_ref[0, :, h * dv:(h + 1) * dv], ones], axis=1)
            acc_ref[h] = jnp.concatenate([alpha, alpha], axis=1) * acc_ref[h] + _dot(p.astype(BF16), v1)
            m_ref[h] = m_new

    @pl.when(ki < qi)
    def _():
        step(False)

    @pl.when(ki == qi)
    def _():
        step(True)
        for h in range(hp):
            acc = acc_ref[h]
            o_ref[0, :, h * dv:(h + 1) * dv] = (acc[:, :dv] / acc[:, dv:]).astype(o_ref.dtype)


def attention(q_arr, q_col, k_arr, k_col, v_arr, v_col, *, heads, dk, dv, aux=None, chunked_mask,
              tile=512, hp=ATTN_HEADS_PER_STEP):
    assert dv == LANES and heads % hp == 0
    b, s, _ = q_arr.shape
    t = _tile(s, tile)
    n = s // t
    pairs = [(i, j) for i in range(n) for j in range(i + 1)]
    qi_tab = jnp.asarray([p[0] for p in pairs], jnp.int32)
    ki_tab = jnp.asarray([p[1] for p in pairs], jnp.int32)

    def spec(width, col, table):
        assert col % (hp * width) == 0
        first = col // (hp * width)
        if table == "q":
            return pl.BlockSpec((1, t, hp * width), lambda i, g, p, qt, kt: (i, qt[p], first + g))
        return pl.BlockSpec((1, t, hp * width), lambda i, g, p, qt, kt: (i, kt[p], first + g))

    if aux is None:
        in_specs = [spec(dk, q_col, "q"), spec(dk, k_col, "k"), spec(dv, v_col, "k")]
        args = [q_arr, k_arr, v_arr]
    else:
        in_specs = [spec(dk, q_col, "q"), spec(LANES, 0, "q"), spec(dk, k_col, "k"), spec(LANES, 0, "k"),
                    spec(dv, v_col, "k")]
        args = [q_arr, aux[0], k_arr, aux[1], v_arr]
    stat = pltpu.VMEM((hp, t, LANES), F32)
    return pl.pallas_call(
        functools.partial(_attn_kernel, has_aux=aux is not None, chunked_mask=chunked_mask, hp=hp, dk=dk,
                          dv=dv, t=t),
        grid_spec=pltpu.PrefetchScalarGridSpec(
            num_scalar_prefetch=2,
            grid=(b, heads // hp, len(pairs)),
            in_specs=in_specs,
            out_specs=spec(dv, 0, "q"),
            scratch_shapes=[stat, pltpu.VMEM((hp, t, dv + LANES), F32)]),
        out_shape=jax.ShapeDtypeStruct((b, s, heads * dv), BF16),
        compiler_params=_params("parallel", "parallel", "arbitrary"),
        name="fox_attention" if aux is not None else "mla_attention",
    )(qi_tab, ki_tab, *args)


def _mla_rope(x, mc, ma, mb):
    return x * mc + pltpu.roll(x, LANES - MLA_ROPE // 2, 1) * ma + pltpu.roll(x, MLA_ROPE // 2, 1) * mb


def _mla_q_kernel(c_ref, g_ref, w_ref, mc_ref, ma_ref, mb_ref, o_ref):
    x = c_ref[...]
    xn = (x * lax.rsqrt(jnp.mean(x * x, axis=-1, keepdims=True) + EPS) * g_ref[...]).astype(BF16)
    q = _dot(xn, w_ref[...])
    scale = (MLA_NOPE + MLA_ROPE) ** -0.5 * LOG2_E
    mc, ma, mb = mc_ref[...], ma_ref[...], mb_ref[...]
    for h in range(MLA_HEADS):
        lo = h * MLA_QK_PAD
        o_ref[:, lo:lo + MLA_NOPE] = (q[:, lo:lo + MLA_NOPE] * scale).astype(o_ref.dtype)
        pe = _mla_rope(q[:, lo + MLA_NOPE:lo + MLA_QK_PAD], mc, ma, mb)
        o_ref[:, lo + MLA_NOPE:lo + MLA_QK_PAD] = (pe * scale).astype(o_ref.dtype)


def mla_queries(za2, q_norm, w_uq_pad, layer, mc, ma, mb, *, bm=512):
    t = za2.shape[0]
    bm = _tile(t, bm, SUBLANES)
    tab = pl.BlockSpec((bm, LANES), lambda i: (i, 0))
    return pl.pallas_call(
        _mla_q_kernel,
        grid=(t // bm,),
        in_specs=[pl.BlockSpec((bm, MLA_Q_LORA), lambda i: (i, A_CQ // MLA_Q_LORA)),
                  pl.BlockSpec((1, MLA_Q_LORA), lambda i: (0, 0)),
                  pl.BlockSpec((None,) + w_uq_pad.shape[1:], lambda i: (layer, 0, 0)),
                  tab, tab, tab],
        out_specs=pl.BlockSpec((bm, MLA_HEADS * MLA_QK_PAD), lambda i: (i, 0)),
        out_shape=jax.ShapeDtypeStruct((t, MLA_HEADS * MLA_QK_PAD), BF16),
        compiler_params=_params("parallel"),
        name="mla_queries",
    )(za2, q_norm.reshape(1, -1), w_uq_pad, mc, ma, mb)


def _mla_kv_kernel(c_ref, kr_ref, g_ref, wk_ref, wv_ref, mc_ref, ma_ref, mb_ref, k_ref, v_ref):
    x = c_ref[...]
    xn = (x * lax.rsqrt(jnp.mean(x * x, axis=-1, keepdims=True) + EPS) * g_ref[...]).astype(BF16)
    k_nope = _dot(xn, wk_ref[...])
    v_ref[...] = _dot(xn, wv_ref[...]).astype(v_ref.dtype)
    k_pe = _mla_rope(kr_ref[...], mc_ref[...], ma_ref[...], mb_ref[...]).astype(k_ref.dtype)
    for h in range(MLA_HEADS):
        lo = h * MLA_QK_PAD
        k_ref[:, lo:lo + MLA_NOPE] = k_nope[:, h * MLA_NOPE:(h + 1) * MLA_NOPE].astype(k_ref.dtype)
        k_ref[:, lo + MLA_NOPE:lo + MLA_QK_PAD] = k_pe


def mla_keys_values(za2, kv_norm, wk, wv, layer, mc, ma, mb, *, bm=512):
    t = za2.shape[0]
    bm = _tile(t, bm, SUBLANES)
    tab = pl.BlockSpec((bm, LANES), lambda i: (i, 0))
    w_spec = pl.BlockSpec((None,) + wk.shape[1:], lambda i: (layer, 0, 0))
    return pl.pallas_call(
        _mla_kv_kernel,
        grid=(t // bm,),
        in_specs=[pl.BlockSpec((bm, MLA_KV_LORA), lambda i: (i, A_CKV // MLA_KV_LORA)),
                  pl.BlockSpec((bm, LANES), lambda i: (i, A_KR // LANES)),
                  pl.BlockSpec((1, MLA_KV_LORA), lambda i: (0, 0)),
                  w_spec, w_spec, tab, tab, tab],
        out_specs=[pl.BlockSpec((bm, MLA_HEADS * MLA_QK_PAD), lambda i: (i, 0)),
                   pl.BlockSpec((bm, MLA_HEADS * MLA_DV), lambda i: (i, 0))],
        out_shape=[jax.ShapeDtypeStruct((t, MLA_HEADS * MLA_QK_PAD), BF16),
                   jax.ShapeDtypeStruct((t, MLA_HEADS * MLA_DV), BF16)],
        compiler_params=_params("parallel"),
        name="mla_keys_values",
    )(za2, za2, kv_norm.reshape(1, -1), wk, wv, mc, ma, mb)


def _prepare_weights(p):
    w_in_t = jnp.swapaxes(p["w_in"], 1, 2)
    depth, width, d = w_in_t.shape
    fq0 = 2 * RET_HEADS * RET_DK + B_FQ
    row_scale = jnp.ones((width, 1), F32).at[fq0:fq0 + FOX_HEADS * FOX_DH].set(FOX_DH ** -0.5 * LOG2_E)
    whole = width // BF16_ROWS * BF16_ROWS
    assert fq0 + FOX_HEADS * FOX_DH <= whole
    w_uq = p["w_uq"].reshape(depth, MLA_Q_LORA, MLA_HEADS, MLA_NOPE + MLA_ROPE)
    w_uq = jnp.pad(w_uq, ((0, 0), (0, 0), (0, 0), (0, MLA_QK_PAD - MLA_NOPE - MLA_ROPE)))
    w_ukv = p["w_ukv"].reshape(depth, MLA_KV_LORA, MLA_HEADS, MLA_NOPE + MLA_DV)
    out = dict(
        w13_first=p["ffn1_w13"][:1].astype(BF16),
        w_in_cast=lambda layer: (w_in_t, layer, whole, row_scale[:whole]),
        w_in_tail=w_in_t[:, whole:].astype(BF16),
        w_uq=w_uq.reshape(depth, MLA_Q_LORA, MLA_HEADS * MLA_QK_PAD).astype(BF16),
        w_uk=w_ukv[..., :MLA_NOPE].reshape(depth, MLA_KV_LORA, MLA_HEADS * MLA_NOPE).astype(BF16),
        w_uv=w_ukv[..., MLA_NOPE:].reshape(depth, MLA_KV_LORA, MLA_HEADS * MLA_DV).astype(BF16))
    return out


def _ffn(h, u, ssq, w13, w13_layer, w2_f32, layer, next_gain, up_casts):
    g, (w2, *up_done) = swiglu_up(u, ssq, w13, w13_layer, casts=[(w2_f32, layer), *up_casts])
    result, _ = residual_matmul(g, w2[None], 0, h, scale=0.5, gain=next_gain, bm=512, bn=512, name="ffn_down")
    return result, up_done


def _regroup_w_in(w_in_b, tail):
    rq_rk_end = 2 * RET_HEADS * RET_DK
    bf_end = rq_rk_end + B_WIDTH
    ff_end = bf_end + FOX_HEADS
    parts = [w_in_b[:rq_rk_end], w_in_b[ff_end:], tail, w_in_b[bf_end:ff_end]]
    used = sum(x.shape[0] for x in parts)
    return jnp.concatenate(parts + [jnp.zeros((A_WIDTH - used, w_in_b.shape[1]), BF16)], axis=0)


def _mixer(h, u, ssq, batch, tables, p, w, w_in_b, w_gate, layer, next_gain, proj_casts, merge_casts):
    rc, rs, mc, ma, mb = tables
    t, d = h.shape
    s = t // batch
    w_in_a = _regroup_w_in(w_in_b, w["w_in_tail"][layer])
    za, (w_up_ret, w_up_fox, w_up_mla) = normed_matmul(
        u, ssq, w_in_a[None], 0, out_dtype=F32, bn=A_TILE, name="in_proj_f32",
        casts=[(p["w_up_ret"], layer), (p["w_up_fox"], layer), (p["w_up_mla"], layer)])
    zb, proj_done = normed_matmul(u, ssq, w_in_b[None], 0, col=2 * RET_HEADS * RET_DK, n=B_WIDTH,
                                  out_dtype=BF16, casts=proj_casts, name="in_proj_bf16")
    za3 = za.reshape(batch, s, A_WIDTH)
    zb3 = zb.reshape(batch, s, B_WIDTH)
    tab3 = lambda x: x.reshape(batch, s, LANES)

    ro = retention(za3, zb3, tab3(rc), tab3(rs), p["ret_norm"][layer])
    fo = attention(zb3, B_FQ, zb3, B_FK, zb3, B_FV, heads=FOX_HEADS, dk=FOX_DH, dv=FOX_DH,
                   aux=forget_bias_lanes(za3, p["b_forget"][layer]), chunked_mask=False)
    qm = mla_queries(za, p["mla_q_norm"][layer], w["w_uq"], layer, mc, ma, mb)
    km, vm = mla_keys_values(za, p["mla_kv_norm"][layer], w["w_uk"], w["w_uv"], layer, mc, ma, mb)
    mo = attention(qm.reshape(batch, s, -1), 0, km.reshape(batch, s, -1), 0, vm.reshape(batch, s, -1), 0,
                   heads=MLA_HEADS, dk=MLA_QK_PAD, dv=MLA_DV, chunked_mask=True)

    merged, (w_out, *merge_done) = gated_merge(
        u, ssq, ro.reshape(t, -1), fo.reshape(t, -1), mo.reshape(t, -1), w_gate, p["b_gate"], layer,
        w_up_ret, w_up_fox, w_up_mla, casts=[(p["w_out"], layer), *merge_casts])
    result, _ = residual_matmul(merged, w_out[None], 0, h, scale=1.0, gain=next_gain, name="out_proj")
    return result, proj_done + merge_done


def kernel(x, positions, ffn1_norm, ffn1_w13, ffn1_w2, mix_norm, w_in, b_forget, ret_norm, mla_q_norm,
           mla_kv_norm, w_uq, w_ukv, w_up_ret, w_up_fox, w_up_mla, w_gate, b_gate, w_out, ffn2_norm,
           ffn2_w13, ffn2_w2, final_norm):
    p = dict(mix_norm=mix_norm, w_in=w_in, b_forget=b_forget, ret_norm=ret_norm, mla_q_norm=mla_q_norm,
             mla_kv_norm=mla_kv_norm, w_uq=w_uq, w_ukv=w_ukv, w_up_ret=w_up_ret, w_up_fox=w_up_fox,
             w_up_mla=w_up_mla, w_gate=w_gate, b_gate=b_gate, w_out=w_out, ffn1_w13=ffn1_w13,
             ffn1_w2=ffn1_w2, ffn2_w13=ffn2_w13, ffn2_w2=ffn2_w2)
    batch, s, d = x.shape
    h = x.reshape(batch * s, d)
    w = _prepare_weights(p)
    tables = rope_tables(positions)
    depth = ffn1_norm.shape[0]
    u, ssq = prenorm(h, ffn1_norm[0])
    w_gate_rows = w_gate.reshape(depth, 3 * d, d)
    w13, w13_layer = w["w13_first"], 0
    for layer in range(depth):
        last = layer + 1 == depth
        (h, u, ssq), (gate, w_in_b) = _ffn(h, u, ssq, w13, w13_layer, ffn1_w2, layer, mix_norm[layer],
                                           [(w_gate_rows, layer), w["w_in_cast"](layer)])
        (h, u, ssq), (w2_b, w13_b) = _mixer(h, u, ssq, batch, tables, p, w, w_in_b, gate.reshape(3, d, d), layer,
                                            ffn2_norm[layer], [(ffn2_w2, layer)], [(ffn2_w13, layer)])
        g, next_w13 = swiglu_up(u, ssq, w13_b[None], 0, casts=[] if last else [(ffn1_w13, layer + 1)])
        result, _ = residual_matmul(g, w2_b[None], 0, h, scale=0.5, gain=None if last else ffn1_norm[layer + 1],
                                    bm=512, bn=512, name="ffn_down")
        if last:
            h = result
        else:
            h, u, ssq = result
            w13, w13_layer = next_w13[0][None], 0
    return rmsnorm(h, final_norm, x.dtype).reshape(batch, s, d)
```

```python
import functools
import math

import jax
import jax.numpy as jnp
from jax import lax
from jax.experimental import pallas as pl
from jax.experimental.pallas import tpu as pltpu

F32 = jnp.float32
BF16 = jnp.bfloat16

CHUNK = 64
RET_HEADS, RET_DK, RET_DV = 8, 128, 256
FOX_HEADS, FOX_DH = 8, 128
MLA_HEADS, MLA_Q_LORA, MLA_KV_LORA = 8, 1024, 512
MLA_NOPE, MLA_ROPE, MLA_DV = 128, 64, 128
MLA_QK_PAD = 256
ROPE_BASE = 10000.0
EPS = 1e-6
NEG_BIG = -1e30
LOG2_E = math.log2(math.e)
N_SPLIT = 3
ATTN_HEADS_PER_STEP = 8

LANES = 128
SUBLANES = 8
BF16_ROWS = 2 * SUBLANES
VMEM_LIMIT_BYTES = 62 * 1024 * 1024

A_RQ, A_RK, A_CQ, A_CKV, A_KR = 0, 1024, 2048, 3072, 3584
A_FF = A_KR + MLA_ROPE
A_WIDTH = 3840
A_TILE = 1280
FF_LANE = A_FF - A_KR
B_RV, B_RG, B_FQ, B_FK, B_FV = 0, 2048, 4096, 5120, 6144
B_WIDTH = 7168


def _tile(dim, pref, quantum=LANES):
    if dim <= pref:
        return dim
    t = (pref // quantum) * quantum
    while t > quantum and dim % t:
        t -= quantum
    assert dim % t == 0, (dim, pref)
    return t


def _params(*sem):
    return pltpu.CompilerParams(dimension_semantics=sem, vmem_limit_bytes=VMEM_LIMIT_BYTES)


def _sigmoid(x):
    return 0.5 + 0.5 * jnp.tanh(0.5 * x)


def _dot(a, b):
    return jnp.dot(a, b, preferred_element_type=F32)


def _dot_nt(a, b):
    return lax.dot_general(a, b, (((1,), (1,)), ((), ())), preferred_element_type=F32)


def _dot_tn(a, b):
    return lax.dot_general(a, b, (((0,), (0,)), ((), ())), preferred_element_type=F32)


def _cast_fields(cast):
    stack, layer, rows, row_scale = (*cast, None, None)[:4]
    return stack, layer, stack.shape[1] if rows is None else rows, row_scale


def _cast_plan(casts, grid):
    steps = math.prod(grid)

    def linear(ids):
        idx = ids[0]
        for extent, i in zip(grid[1:], ids[1:]):
            idx = idx * extent + i
        return idx

    src_specs, scale_specs, out_specs, out_shapes, sources, scales = [], [], [], [], [], []
    for cast in casts:
        stack, layer, rows, row_scale = _cast_fields(cast)
        cols = stack.shape[2]
        blocks = max(n for n in range(1, steps + 1) if rows % n == 0 and (rows // n) % BF16_ROWS == 0)
        rb, rep = rows // blocks, steps // blocks
        block_of = lambda ids, rep=rep, last=blocks - 1: jnp.minimum(linear(ids) // rep, last)
        src_specs.append(pl.BlockSpec((None, rb, cols),
                                      lambda *ids, layer=layer, block_of=block_of: (layer, block_of(ids), 0)))
        out_specs.append(pl.BlockSpec((rb, cols), lambda *ids, block_of=block_of: (block_of(ids), 0)))
        out_shapes.append(jax.ShapeDtypeStruct((rows, cols), BF16))
        sources.append(stack)
        if row_scale is not None:
            scale_specs.append(pl.BlockSpec((rb, 1), lambda *ids, block_of=block_of: (block_of(ids), 0)))
            scales.append(row_scale)
    return src_specs + scale_specs, out_specs, out_shapes, sources + scales


def _hosting(body, n_in, n_out, casts):
    if not casts:
        return body
    scaled = [_cast_fields(c)[3] is not None for c in casts]
    n_cast, n_scale = len(casts), sum(scaled)

    def kernel(*refs):
        ins, rest = refs[:n_in], refs[n_in:]
        srcs, rest = rest[:n_cast], rest[n_cast:]
        scale_refs, rest = list(rest[:n_scale]), rest[n_scale:]
        outs, rest = rest[:n_out], rest[n_out:]
        dsts, scratch = rest[:n_cast], rest[n_cast:]
        body(*ins, *outs, *scratch)
        for src, dst, has_scale in zip(srcs, dsts, scaled):
            x = src[...]
            if has_scale:
                x = x * scale_refs.pop(0)[...]
            dst[...] = x.astype(dst.dtype)

    return kernel


def _rmsnorm_kernel(x_ref, g_ref, o_ref):
    x = x_ref[...]
    ms = jnp.mean(x * x, axis=-1, keepdims=True)
    o_ref[...] = (x * lax.rsqrt(ms + EPS) * g_ref[...]).astype(o_ref.dtype)


def rmsnorm(x, g, out_dtype):
    m, d = x.shape
    bm = _tile(m, 512, SUBLANES)
    return pl.pallas_call(
        _rmsnorm_kernel,
        grid=(m // bm,),
        in_specs=[pl.BlockSpec((bm, d), lambda i: (i, 0)),
                  pl.BlockSpec((1, d), lambda i: (0, 0))],
        out_specs=pl.BlockSpec((bm, d), lambda i: (i, 0)),
        out_shape=jax.ShapeDtypeStruct((m, d), out_dtype),
        compiler_params=_params("parallel"),
        name="rmsnorm",
    )(x, g.reshape(1, d))


def _lane_partial_ssq(x):
    sq = x * x
    part = sq[:, :LANES]
    for c in range(1, x.shape[1] // LANES):
        part = part + sq[:, c * LANES:(c + 1) * LANES]
    return part


def _row_rstd(ssq_ref, inv_d):
    return lax.rsqrt(jnp.sum(ssq_ref[...], axis=1, keepdims=True) * inv_d + EPS)


def _prenorm_kernel(x_ref, g_ref, u_ref, ssq_ref):
    x = x_ref[...]
    u_ref[...] = (x * g_ref[...]).astype(u_ref.dtype)
    ssq_ref[...] = _lane_partial_ssq(x)


def prenorm(x, g):
    m, d = x.shape
    bm = _tile(m, 512, SUBLANES)
    return pl.pallas_call(
        _prenorm_kernel,
        grid=(m // bm,),
        in_specs=[pl.BlockSpec((bm, d), lambda i: (i, 0)),
                  pl.BlockSpec((1, d), lambda i: (0, 0))],
        out_specs=[pl.BlockSpec((bm, d), lambda i: (i, 0)),
                   pl.BlockSpec((bm, LANES), lambda i: (i, 0))],
        out_shape=[jax.ShapeDtypeStruct((m, d), BF16), jax.ShapeDtypeStruct((m, LANES), F32)],
        compiler_params=_params("parallel"),
        name="prenorm",
    )(x, g.reshape(1, d))


def _proj_kernel(a_ref, ssq_ref, wt_ref, o_ref, *, inv_d):
    acc = _dot_nt(a_ref[...], wt_ref[...])
    o_ref[...] = (acc * _row_rstd(ssq_ref, inv_d)).astype(o_ref.dtype)


def normed_matmul(a, ssq, wt, layer, *, col=0, n=None, out_dtype, casts=(), bm=1024, bn=1024, name):
    m, kd = a.shape
    n = wt.shape[1] if n is None else n
    bm, bn = _tile(m, bm, SUBLANES), _tile(math.gcd(n, col) if col else n, bn)
    first = col // bn
    grid = (m // bm, n // bn)
    c_in, c_out, c_shape, c_args = _cast_plan(casts, grid)
    out, *casted = pl.pallas_call(
        _hosting(functools.partial(_proj_kernel, inv_d=1.0 / kd), 3, 1, casts),
        grid=grid,
        in_specs=[pl.BlockSpec((bm, kd), lambda i, j: (i, 0)),
                  pl.BlockSpec((bm, LANES), lambda i, j: (i, 0)),
                  pl.BlockSpec((None, bn, kd), lambda i, j: (layer, first + j, 0))] + c_in,
        out_specs=[pl.BlockSpec((bm, bn), lambda i, j: (i, j))] + c_out,
        out_shape=[jax.ShapeDtypeStruct((m, n), out_dtype)] + c_shape,
        compiler_params=_params("arbitrary", "arbitrary"),
        name=name,
    )(a, ssq, wt, *c_args)
    return out, casted


def _residual_kernel(a_ref, w_ref, r_ref, *refs, nk, scale, emit_norm):
    if emit_norm:
        g_ref, o_ref, u_ref, ssq_ref = refs
    else:
        o_ref, = refs

    def finish(h):
        o_ref[...] = h
        if emit_norm:
            u_ref[...] = (h * g_ref[...]).astype(u_ref.dtype)
            row_ssq = _lane_partial_ssq(h)
            j = pl.program_id(1)

            @pl.when(j == 0)
            def _():
                ssq_ref[...] = row_ssq

            @pl.when(j > 0)
            def _():
                ssq_ref[...] += row_ssq

    part = scale * _dot(a_ref[...], w_ref[...])
    if nk == 1:
        finish(r_ref[...] + part)
        return
    k = pl.program_id(2)

    @pl.when(k == 0)
    def _():
        o_ref[...] = r_ref[...] + part

    @pl.when(jnp.logical_and(k > 0, k < nk - 1))
    def _():
        o_ref[...] += part

    @pl.when(k == nk - 1)
    def _():
        finish(o_ref[...] + part)


def residual_matmul(a, w, layer, res, *, scale, gain=None, casts=(), bm=1024, bn=1024, bk=None, name):
    m, kd = a.shape
    n = w.shape[2]
    bm, bn = _tile(m, bm, SUBLANES), _tile(n, bn)
    bk = kd if bk is None else _tile(kd, bk)
    nk = kd // bk
    tile = pl.BlockSpec((bm, bn), lambda i, j, k: (i, j))
    in_specs = [pl.BlockSpec((bm, bk), lambda i, j, k: (i, k)),
                pl.BlockSpec((None, bk, bn), lambda i, j, k: (layer, k, j)),
                tile]
    args = [a, w, res]
    out_specs, out_shape = [tile], [jax.ShapeDtypeStruct((m, n), F32)]
    if gain is not None:
        in_specs.append(pl.BlockSpec((1, bn), lambda i, j, k: (0, j)))
        args.append(gain.reshape(1, n))
        out_specs += [tile, pl.BlockSpec((bm, LANES), lambda i, j, k: (i, 0))]
        out_shape += [jax.ShapeDtypeStruct((m, n), BF16), jax.ShapeDtypeStruct((m, LANES), F32)]
    n_out = len(out_specs)
    grid = (m // bm, n // bn, nk)
    c_in, c_out, c_shape, c_args = _cast_plan(casts, grid)
    outs = pl.pallas_call(
        _hosting(functools.partial(_residual_kernel, nk=nk, scale=scale, emit_norm=gain is not None),
                 len(args), n_out, casts),
        grid=grid,
        in_specs=in_specs + c_in,
        out_specs=out_specs + c_out,
        out_shape=out_shape + c_shape,
        compiler_params=_params("parallel", "arbitrary", "arbitrary"),
        name=name,
    )(*args, *c_args)
    result = outs[0] if gain is None else tuple(outs[:n_out])
    return result, list(outs[n_out:])


def _swiglu_kernel(u_ref, ssq_ref, w1_ref, w3_ref, o_ref, *, inv_d):
    u = u_ref[...]
    rstd = _row_rstd(ssq_ref, inv_d)
    a = _dot(u, w1_ref[...]) * rstd
    b = _dot(u, w3_ref[...]) * rstd
    o_ref[...] = (a * _sigmoid(a) * b).astype(o_ref.dtype)


def swiglu_up(u, ssq, w13, layer, *, casts=(), bm=1024, bn=512):
    m, d = u.shape
    f = w13.shape[2] // 2
    bm, bn = _tile(m, bm, SUBLANES), _tile(f, bn)
    nf = f // bn
    grid = (m // bm, nf)
    c_in, c_out, c_shape, c_args = _cast_plan(casts, grid)
    out, *casted = pl.pallas_call(
        _hosting(functools.partial(_swiglu_kernel, inv_d=1.0 / d), 4, 1, casts),
        grid=grid,
        in_specs=[pl.BlockSpec((bm, d), lambda i, j: (i, 0)),
                  pl.BlockSpec((bm, LANES), lambda i, j: (i, 0)),
                  pl.BlockSpec((None, d, bn), lambda i, j: (layer, 0, j)),
                  pl.BlockSpec((None, d, bn), lambda i, j: (layer, 0, j + nf))] + c_in,
        out_specs=[pl.BlockSpec((bm, bn), lambda i, j: (i, j))] + c_out,
        out_shape=[jax.ShapeDtypeStruct((m, f), BF16)] + c_shape,
        compiler_params=_params("arbitrary", "arbitrary"),
        name="swiglu_up",
    )(u, ssq, w13, w13, *c_args)
    return out, casted


def _merge_kernel(u_ref, ssq_ref, ro_ref, fo_ref, mo_ref, wg_ref, bg_ref, wr_ref, wf_ref, wm_ref, o_ref, *,
                  inv_d):
    u = u_ref[...]
    rstd = _row_rstd(ssq_ref, inv_d)
    total = None
    for i, (x_ref, w_ref) in enumerate(((ro_ref, wr_ref), (fo_ref, wf_ref), (mo_ref, wm_ref))):
        gate = jax.nn.sigmoid(_dot(u, wg_ref[i]) * rstd + bg_ref[i:i + 1, :])
        term = gate * _dot(x_ref[...], w_ref[...])
        total = term if total is None else total + term
    o_ref[...] = total.astype(o_ref.dtype)


def gated_merge(u, ssq, ro, fo, mo, w_gate, b_gate, layer, w_up_ret, w_up_fox, w_up_mla, *, casts=(),
                bm=512, bn=512):
    m, d = u.shape
    n = w_gate.shape[-1]
    bm, bn = _tile(m, bm, SUBLANES), _tile(n, bn)
    row = lambda width: pl.BlockSpec((bm, width), lambda i, j: (i, 0))
    col = lambda depth: pl.BlockSpec((depth, bn), lambda i, j: (0, j))
    grid = (m // bm, n // bn)
    c_in, c_out, c_shape, c_args = _cast_plan(casts, grid)
    out, *casted = pl.pallas_call(
        _hosting(functools.partial(_merge_kernel, inv_d=1.0 / d), 10, 1, casts),
        grid=grid,
        in_specs=[row(d), row(LANES), row(ro.shape[1]), row(fo.shape[1]), row(mo.shape[1]),
                  pl.BlockSpec((3, d, bn), lambda i, j: (0, 0, j)),
                  pl.BlockSpec((None, 3, bn), lambda i, j: (layer, 0, j)),
                  col(ro.shape[1]), col(fo.shape[1]), col(mo.shape[1])] + c_in,
        out_specs=[pl.BlockSpec((bm, bn), lambda i, j: (i, j))] + c_out,
        out_shape=[jax.ShapeDtypeStruct((m, n), BF16)] + c_shape,
        compiler_params=_params("arbitrary", "arbitrary"),
        name="gated_merge",
    )(u, ssq, ro, fo, mo, w_gate, b_gate, w_up_ret, w_up_fox, w_up_mla, *c_args)
    return out, casted


def _rope_table_kernel(pos_ref, inv_ref, coef_ref, rc_ref, rs_ref, mc_ref, ma_ref, mb_ref):
    p = pos_ref[...].astype(F32)
    ang_r = p * inv_ref[0:1, :]
    ang_m = p * inv_ref[1:2, :]
    rc_ref[...] = jnp.cos(ang_r)
    rs_ref[...] = jnp.sin(ang_r) * coef_ref[0:1, :]
    sin_m = jnp.sin(ang_m)
    mc_ref[...] = jnp.cos(ang_m) * coef_ref[1:2, :]
    ma_ref[...] = sin_m * coef_ref[2:3, :]
    mb_ref[...] = sin_m * coef_ref[3:4, :]


def rope_tables(positions, casts=()):
    t = positions.size
    h_r, h_m = RET_DK // 2, MLA_ROPE // 2
    inv_r = ROPE_BASE ** (-jnp.arange(h_r, dtype=F32) / h_r)
    inv_m = ROPE_BASE ** (-jnp.arange(h_m, dtype=F32) / h_m)
    zeros_m = jnp.zeros((LANES - 2 * h_m,), F32)
    ones_m = jnp.ones((h_m,), F32)
    inv = jnp.stack([jnp.concatenate([inv_r, inv_r]),
                     jnp.concatenate([inv_m, inv_m, zeros_m])])
    coef = jnp.stack([
        jnp.concatenate([-jnp.ones((h_r,), F32), jnp.ones((h_r,), F32)]),
        jnp.concatenate([ones_m, ones_m, zeros_m]),
        jnp.concatenate([-ones_m, 0 * ones_m, zeros_m]),
        jnp.concatenate([0 * ones_m, ones_m, zeros_m]),
    ])
    bm = _tile(t, 256, SUBLANES)
    tab = jax.ShapeDtypeStruct((t, LANES), F32)
    grid = (t // bm,)
    c_in, c_out, c_shape, c_args = _cast_plan(casts, grid)
    *tables, = pl.pallas_call(
        _hosting(_rope_table_kernel, 3, 5, casts),
        grid=grid,
        in_specs=[pl.BlockSpec((bm, 1), lambda i: (i, 0)),
                  pl.BlockSpec((2, LANES), lambda i: (0, 0)),
                  pl.BlockSpec((4, LANES), lambda i: (0, 0))] + c_in,
        out_specs=[pl.BlockSpec((bm, LANES), lambda i: (i, 0))] * 5 + c_out,
        out_shape=[tab] * 5 + c_shape,
        compiler_params=_params("arbitrary"),
        name="rope_tables",
    )(positions.reshape(t, 1), inv, coef, *c_args)
    return tables[:5], tables[5:]


def _ret_log_gamma(h):
    return math.log1p(-2.0 ** (-5.0 - h))


def _retention_kernel(q_ref, k_ref, v_ref, g_ref, rc_ref, rs_ref, norm_ref, o_ref, state_ref, decay_ref,
                      *, blk):
    first = jnp.logical_and(pl.program_id(0) == 0, pl.program_id(1) == 0)

    @pl.when(first)
    def _():
        row = lax.broadcasted_iota(jnp.int32, (blk, blk), 0)
        col = lax.broadcasted_iota(jnp.int32, (blk, blk), 1)
        dist = jnp.abs(row - col).astype(F32)
        visible = (col // CHUNK) <= (row // CHUNK)
        for h in range(RET_HEADS):
            decay_ref[h] = jnp.where(visible, jnp.exp(_ret_log_gamma(h) * dist), 0.0)

    @pl.when(pl.program_id(1) == 0)
    def _():
        state_ref[...] = jnp.zeros_like(state_ref)

    rc = rc_ref[0]
    rs = rs_ref[0]
    idx = lax.broadcasted_iota(jnp.int32, (blk, 1), 0).astype(F32)
    for h in range(RET_HEADS):
        lg = _ret_log_gamma(h)
        qk = slice(h * RET_DK, (h + 1) * RET_DK)
        vv = slice(h * RET_DV, (h + 1) * RET_DV)
        q = q_ref[0, :, qk]
        k = k_ref[0, :, qk]
        q = (q * rc + pltpu.roll(q, RET_DK // 2, 1) * rs) * (RET_DK ** -0.5)
        k = k * rc + pltpu.roll(k, RET_DK // 2, 1) * rs
        v = v_ref[0, :, vv]
        state = state_ref[h]
        scores = _dot_nt(q.astype(BF16), k.astype(BF16)) * decay_ref[h]
        q_in = (q * jnp.exp(lg * (idx + 1.0))).astype(BF16)
        out = _dot(scores.astype(BF16), v) + _dot(q_in, state.astype(BF16))
        k_out = (k * jnp.exp(lg * (blk - 1.0 - idx))).astype(BF16)
        state_ref[h] = state * math.exp(lg * blk) + _dot_tn(k_out, v)
        out = out * lax.rsqrt(jnp.mean(out * out, axis=-1, keepdims=True) + EPS)
        gate = g_ref[0, :, vv].astype(F32)
        out = out * norm_ref[:, vv] * (gate * _sigmoid(gate))
        o_ref[0, :, vv] = out.astype(o_ref.dtype)


def retention(za, zb, rc, rs, ret_norm, *, blk=256):
    b, s, _ = za.shape
    blk = _tile(s, blk, CHUNK)
    qw, vw = RET_HEADS * RET_DK, RET_HEADS * RET_DV
    return pl.pallas_call(
        functools.partial(_retention_kernel, blk=blk),
        grid=(b, s // blk),
        in_specs=[pl.BlockSpec((1, blk, qw), lambda i, j: (i, j, A_RQ // qw)),
                  pl.BlockSpec((1, blk, qw), lambda i, j: (i, j, A_RK // qw)),
                  pl.BlockSpec((1, blk, vw), lambda i, j: (i, j, B_RV // vw)),
                  pl.BlockSpec((1, blk, vw), lambda i, j: (i, j, B_RG // vw)),
                  pl.BlockSpec((1, blk, LANES), lambda i, j: (i, j, 0)),
                  pl.BlockSpec((1, blk, LANES), lambda i, j: (i, j, 0)),
                  pl.BlockSpec((1, vw), lambda i, j: (0, 0))],
        out_specs=pl.BlockSpec((1, blk, vw), lambda i, j: (i, j, 0)),
        out_shape=jax.ShapeDtypeStruct((b, s, vw), BF16),
        scratch_shapes=[pltpu.VMEM((RET_HEADS, RET_DK, RET_DV), F32),
                        pltpu.VMEM((RET_HEADS, blk, blk), F32)],
        compiler_params=_params("arbitrary", "arbitrary"),
        name="retention",
    )(za, za, zb, zb, rc, rs, ret_norm.reshape(1, vw))


def _split3(x):
    hi = x.astype(BF16).astype(F32)
    rest = x - hi
    mid = rest.astype(BF16).astype(F32)
    lo = (rest - mid).astype(BF16).astype(F32)
    return hi, mid, lo


def _forget_cumsum_kernel(z_ref, bias_ref, qa_ref, ka_ref, carry_ref, *, rows):
    @pl.when(pl.program_id(1) == 0)
    def _():
        carry_ref[...] = jnp.zeros_like(carry_ref)

    lane = lax.broadcasted_iota(jnp.int32, (rows, LANES), 1)
    is_gate = jnp.logical_and(lane >= FF_LANE, lane < FF_LANE + FOX_HEADS)
    x = z_ref[0] + bias_ref[...]
    log_f = jnp.minimum(x, 0.0) - jnp.log(1.0 + jnp.exp(-jnp.abs(x)))
    log_f = jnp.where(is_gate, log_f * LOG2_E, 0.0)
    tri = (lax.broadcasted_iota(jnp.int32, (rows, rows), 0)
           >= lax.broadcasted_iota(jnp.int32, (rows, rows), 1)).astype(BF16)
    cum = carry_ref[...]
    for piece in _split3(log_f):
        cum = cum + _dot(tri, piece.astype(BF16))
    carry_ref[...] = cum[rows - 1:rows, :]
    one = jnp.where(lane < 2 * N_SPLIT, 1.0, 0.0)
    for h in range(FOX_HEADS):
        col = jnp.sum(jnp.where(lane == FF_LANE + h, cum, 0.0), axis=1, keepdims=True)
        pieces = _split3(jnp.broadcast_to(col, (rows, LANES)))
        qa = jnp.where(lane < N_SPLIT, 0.0, one)
        ka = jnp.where(lane < N_SPLIT, one, 0.0)
        for i, piece in enumerate(pieces):
            qa = jnp.where(lane == i, piece, qa)
            ka = jnp.where(lane == N_SPLIT + i, -piece, ka)
        qa_ref[0, :, h * LANES:(h + 1) * LANES] = qa.astype(qa_ref.dtype)
        ka_ref[0, :, h * LANES:(h + 1) * LANES] = ka.astype(ka_ref.dtype)


def forget_bias_lanes(za, b_forget, *, rows=512):
    b, s, _ = za.shape
    rows = _tile(s, rows)
    bias = jnp.zeros((1, LANES), F32).at[0, FF_LANE:FF_LANE + FOX_HEADS].set(b_forget)
    aux = jax.ShapeDtypeStruct((b, s, FOX_HEADS * LANES), BF16)
    return pl.pallas_call(
        functools.partial(_forget_cumsum_kernel, rows=rows),
        grid=(b, s // rows),
        in_specs=[pl.BlockSpec((1, rows, LANES), lambda i, j: (i, j, A_KR // LANES)),
                  pl.BlockSpec((1, LANES), lambda i, j: (0, 0))],
        out_specs=[pl.BlockSpec((1, rows, FOX_HEADS * LANES), lambda i, j: (i, j, 0))] * 2,
        out_shape=[aux, aux],
        scratch_shapes=[pltpu.VMEM((1, LANES), F32)],
        compiler_params=_params("arbitrary", "arbitrary"),
        name="forget_bias_lanes",
    )(za, bias)


def _attn_kernel(qi_ref, ki_ref, *refs, has_aux, chunked_mask, hp, dk, dv, t):
    if has_aux:
        q_ref, qa_ref, k_ref, ka_ref, v_ref, o_ref, m_ref, acc_ref = refs
    else:
        q_ref, k_ref, v_ref, o_ref, m_ref, acc_ref = refs
    pair = pl.program_id(2)
    qi, ki = qi_ref[pair], ki_ref[pair]

    @pl.when(ki == 0)
    def _():
        m_ref[...] = jnp.full_like(m_ref, NEG_BIG)
        acc_ref[...] = jnp.zeros_like(acc_ref)

    ones = jnp.ones((t, LANES), BF16)

    def step(diagonal):
        if diagonal:
            row = lax.broadcasted_iota(jnp.int32, (t, t), 0)
            col = lax.broadcasted_iota(jnp.int32, (t, t), 1)
            keep = (col // CHUNK) <= (row // CHUNK) if chunked_mask else col <= row
        for h in range(hp):
            q = q_ref[0, :, h * dk:(h + 1) * dk]
            k = k_ref[0, :, h * dk:(h + 1) * dk]
            if has_aux:
                q = jnp.concatenate([q, qa_ref[0, :, h * LANES:(h + 1) * LANES]], axis=1)
                k = jnp.concatenate([k, ka_ref[0, :, h * LANES:(h + 1) * LANES]], axis=1)
            s = _dot_nt(q, k)
            if diagonal:
                s = jnp.where(keep, s, NEG_BIG)
            m_old = m_ref[h]
            m_new = jnp.maximum(m_old, jnp.max(s, axis=1, keepdims=True))
            alpha = jnp.exp2(m_old - m_new)
            p = jnp.exp2(s - jnp.concatenate([m_new] * (t // LANES), axis=1))
            v1 = jnp.concatenate([v_ref[0, :, h * dv:(h + 1) * dv], ones], axis=1)
            acc_ref[h] = jnp.concatenate([alpha, alpha], axis=1) * acc_ref[h] + _dot(p.astype(BF16), v1)
            m_ref[h] = m_new

    @pl.when(ki < qi)
    def _():
        step(False)

    @pl.when(ki == qi)
    def _():
        step(True)
        for h in range(hp):
            acc = acc_ref[h]
            o_ref[0, :, h * dv:(h + 1) * dv] = (acc[:, :dv] / acc[:, dv:]).astype(o_ref.dtype)


def attention(q_arr, q_col, k_arr, k_col, v_arr, v_col, *, heads, dk, dv, aux=None, chunked_mask,
              tile=512, hp=ATTN_HEADS_PER_STEP):
    assert dv == LANES and heads % hp == 0
    b, s, _ = q_arr.shape
    t = _tile(s, tile)
    n = s // t
    pairs = [(i, j) for i in range(n) for j in range(i + 1)]
    qi_tab = jnp.asarray([p[0] for p in pairs], jnp.int32)
    ki_tab = jnp.asarray([p[1] for p in pairs], jnp.int32)

    def spec(width, col, table):
        assert col % (hp * width) == 0
        first = col // (hp * width)
        if table == "q":
            return pl.BlockSpec((1, t, hp * width), lambda i, g, p, qt, kt: (i, qt[p], first + g))
        return pl.BlockSpec((1, t, hp * width), lambda i, g, p, qt, kt: (i, kt[p], first + g))

    if aux is None:
        in_specs = [spec(dk, q_col, "q"), spec(dk, k_col, "k"), spec(dv, v_col, "k")]
        args = [q_arr, k_arr, v_arr]
    else:
        in_specs = [spec(dk, q_col, "q"), spec(LANES, 0, "q"), spec(dk, k_col, "k"), spec(LANES, 0, "k"),
                    spec(dv, v_col, "k")]
        args = [q_arr, aux[0], k_arr, aux[1], v_arr]
    stat = pltpu.VMEM((hp, t, LANES), F32)
    return pl.pallas_call(
        functools.partial(_attn_kernel, has_aux=aux is not None, chunked_mask=chunked_mask, hp=hp, dk=dk,
                          dv=dv, t=t),
        grid_spec=pltpu.PrefetchScalarGridSpec(
            num_scalar_prefetch=2,
            grid=(b, heads // hp, len(pairs)),
            in_specs=in_specs,
            out_specs=spec(dv, 0, "q"),
            scratch_shapes=[stat, pltpu.VMEM((hp, t, dv + LANES), F32)]),
        out_shape=jax.ShapeDtypeStruct((b, s, heads * dv), BF16),
        compiler_params=_params("parallel", "parallel", "arbitrary"),
        name="fox_attention" if aux is not None else "mla_attention",
    )(qi_tab, ki_tab, *args)


def _mla_rope(x, mc, ma, mb):
    return x * mc + pltpu.roll(x, LANES - MLA_ROPE // 2, 1) * ma + pltpu.roll(x, MLA_ROPE // 2, 1) * mb


def _mla_q_kernel(c_ref, g_ref, w_ref, mc_ref, ma_ref, mb_ref, o_ref):
    x = c_ref[...]
    xn = (x * lax.rsqrt(jnp.mean(x * x, axis=-1, keepdims=True) + EPS) * g_ref[...]).astype(BF16)
    q = _dot(xn, w_ref[...])
    scale = (MLA_NOPE + MLA_ROPE) ** -0.5 * LOG2_E
    mc, ma, mb = mc_ref[...], ma_ref[...], mb_ref[...]
    for h in range(MLA_HEADS):
        lo = h * MLA_QK_PAD
        o_ref[:, lo:lo + MLA_NOPE] = (q[:, lo:lo + MLA_NOPE] * scale).astype(o_ref.dtype)
        pe = _mla_rope(q[:, lo + MLA_NOPE:lo + MLA_QK_PAD], mc, ma, mb)
        o_ref[:, lo + MLA_NOPE:lo + MLA_QK_PAD] = (pe * scale).astype(o_ref.dtype)


def mla_queries(za2, q_norm, w_uq_pad, layer, mc, ma, mb, *, bm=512):
    t = za2.shape[0]
    bm = _tile(t, bm, SUBLANES)
    tab = pl.BlockSpec((bm, LANES), lambda i: (i, 0))
    return pl.pallas_call(
        _mla_q_kernel,
        grid=(t // bm,),
        in_specs=[pl.BlockSpec((bm, MLA_Q_LORA), lambda i: (i, A_CQ // MLA_Q_LORA)),
                  pl.BlockSpec((1, MLA_Q_LORA), lambda i: (0, 0)),
                  pl.BlockSpec((None,) + w_uq_pad.shape[1:], lambda i: (layer, 0, 0)),
                  tab, tab, tab],
        out_specs=pl.BlockSpec((bm, MLA_HEADS * MLA_QK_PAD), lambda i: (i, 0)),
        out_shape=jax.ShapeDtypeStruct((t, MLA_HEADS * MLA_QK_PAD), BF16),
        compiler_params=_params("parallel"),
        name="mla_queries",
    )(za2, q_norm.reshape(1, -1), w_uq_pad, mc, ma, mb)


def _mla_kv_kernel(c_ref, kr_ref, g_ref, wk_ref, wv_ref, mc_ref, ma_ref, mb_ref, k_ref, v_ref):
    x = c_ref[...]
    xn = (x * lax.rsqrt(jnp.mean(x * x, axis=-1, keepdims=True) + EPS) * g_ref[...]).astype(BF16)
    k_nope = _dot(xn, wk_ref[...])
    v_ref[...] = _dot(xn, wv_ref[...]).astype(v_ref.dtype)
    k_pe = _mla_rope(kr_ref[...], mc_ref[...], ma_ref[...], mb_ref[...]).astype(k_ref.dtype)
    for h in range(MLA_HEADS):
        lo = h * MLA_QK_PAD
        k_ref[:, lo:lo + MLA_NOPE] = k_nope[:, h * MLA_NOPE:(h + 1) * MLA_NOPE].astype(k_ref.dtype)
        k_ref[:, lo + MLA_NOPE:lo + MLA_QK_PAD] = k_pe


def mla_keys_values(za2, kv_norm, wk, wv, layer, mc, ma, mb, *, bm=512):
    t = za2.shape[0]
    bm = _tile(t, bm, SUBLANES)
    tab = pl.BlockSpec((bm, LANES), lambda i: (i, 0))
    w_spec = pl.BlockSpec((None,) + wk.shape[1:], lambda i: (layer, 0, 0))
    return pl.pallas_call(
        _mla_kv_kernel,
        grid=(t // bm,),
        in_specs=[pl.BlockSpec((bm, MLA_KV_LORA), lambda i: (i, A_CKV // MLA_KV_LORA)),
                  pl.BlockSpec((bm, LANES), lambda i: (i, A_KR // LANES)),
                  pl.BlockSpec((1, MLA_KV_LORA), lambda i: (0, 0)),
                  w_spec, w_spec, tab, tab, tab],
        out_specs=[pl.BlockSpec((bm, MLA_HEADS * MLA_QK_PAD), lambda i: (i, 0)),
                   pl.BlockSpec((bm, MLA_HEADS * MLA_DV), lambda i: (i, 0))],
        out_shape=[jax.ShapeDtypeStruct((t, MLA_HEADS * MLA_QK_PAD), BF16),
                   jax.ShapeDtypeStruct((t, MLA_HEADS * MLA_DV), BF16)],
        compiler_params=_params("parallel"),
        name="mla_keys_values",
    )(za2, za2, kv_norm.reshape(1, -1), wk, wv, mc, ma, mb)


def _prepare_weights(p):
    w_in_t = jnp.swapaxes(p["w_in"], 1, 2)
    depth, width, d = w_in_t.shape
    fq0 = 2 * RET_HEADS * RET_DK + B_FQ
    row_scale = jnp.ones((width, 1), F32).at[fq0:fq0 + FOX_HEADS * FOX_DH].set(FOX_DH ** -0.5 * LOG2_E)
    whole = width // BF16_ROWS * BF16_ROWS
    assert fq0 + FOX_HEADS * FOX_DH <= whole
    w_uq = p["w_uq"].reshape(depth, MLA_Q_LORA, MLA_HEADS, MLA_NOPE + MLA_ROPE)
    w_uq = jnp.pad(w_uq, ((0, 0), (0, 0), (0, 0), (0, MLA_QK_PAD - MLA_NOPE - MLA_ROPE)))
    w_ukv = p["w_ukv"].reshape(depth, MLA_KV_LORA, MLA_HEADS, MLA_NOPE + MLA_DV)
    out = dict(
        w_in_cast=lambda layer: (w_in_t, layer, whole, row_scale[:whole]),
        w_in_tail=lax.optimization_barrier(w_in_t[:, whole:]).astype(BF16),
        w_uq=w_uq.reshape(depth, MLA_Q_LORA, MLA_HEADS * MLA_QK_PAD).astype(BF16),
        w_uk=w_ukv[..., :MLA_NOPE].reshape(depth, MLA_KV_LORA, MLA_HEADS * MLA_NOPE).astype(BF16),
        w_uv=w_ukv[..., MLA_NOPE:].reshape(depth, MLA_KV_LORA, MLA_HEADS * MLA_DV).astype(BF16))
    return out


def _ffn(h, u, ssq, w13, w13_layer, w2_f32, layer, next_gain, up_casts):
    g, (w2, *up_done) = swiglu_up(u, ssq, w13, w13_layer, casts=[(w2_f32, layer), *up_casts])
    result, _ = residual_matmul(g, w2[None], 0, h, scale=0.5, gain=next_gain, bm=512, bn=512, name="ffn_down")
    return result, up_done


def _regroup_w_in(w_in_b, tail):
    rq_rk_end = 2 * RET_HEADS * RET_DK
    bf_end = rq_rk_end + B_WIDTH
    ff_end = bf_end + FOX_HEADS
    parts = [w_in_b[:rq_rk_end], w_in_b[ff_end:], tail, w_in_b[bf_end:ff_end]]
    used = sum(x.shape[0] for x in parts)
    return jnp.concatenate(parts + [jnp.zeros((A_WIDTH - used, w_in_b.shape[1]), BF16)], axis=0)


def _mixer(h, u, ssq, batch, tables, p, w, w_in_b, w_gate, layer, next_gain, proj_casts, merge_casts):
    rc, rs, mc, ma, mb = tables
    t, d = h.shape
    s = t // batch
    w_in_a = _regroup_w_in(w_in_b, w["w_in_tail"][layer])
    za, (w_up_ret, w_up_fox, w_up_mla) = normed_matmul(
        u, ssq, w_in_a[None], 0, out_dtype=F32, bn=A_TILE, name="in_proj_f32",
        casts=[(p["w_up_ret"], layer), (p["w_up_fox"], layer), (p["w_up_mla"], layer)])
    zb, proj_done = normed_matmul(u, ssq, w_in_b[None], 0, col=2 * RET_HEADS * RET_DK, n=B_WIDTH,
                                  out_dtype=BF16, casts=proj_casts, name="in_proj_bf16")
    za3 = za.reshape(batch, s, A_WIDTH)
    zb3 = zb.reshape(batch, s, B_WIDTH)
    tab3 = lambda x: x.reshape(batch, s, LANES)

    ro = retention(za3, zb3, tab3(rc), tab3(rs), p["ret_norm"][layer])
    fo = attention(zb3, B_FQ, zb3, B_FK, zb3, B_FV, heads=FOX_HEADS, dk=FOX_DH, dv=FOX_DH,
                   aux=forget_bias_lanes(za3, p["b_forget"][layer]), chunked_mask=False)
    qm = mla_queries(za, p["mla_q_norm"][layer], w["w_uq"], layer, mc, ma, mb)
    km, vm = mla_keys_values(za, p["mla_kv_norm"][layer], w["w_uk"], w["w_uv"], layer, mc, ma, mb)
    mo = attention(qm.reshape(batch, s, -1), 0, km.reshape(batch, s, -1), 0, vm.reshape(batch, s, -1), 0,
                   heads=MLA_HEADS, dk=MLA_QK_PAD, dv=MLA_DV, chunked_mask=True)

    merged, (w_out, *merge_done) = gated_merge(
        u, ssq, ro.reshape(t, -1), fo.reshape(t, -1), mo.reshape(t, -1), w_gate, p["b_gate"], layer,
        w_up_ret, w_up_fox, w_up_mla, casts=[(p["w_out"], layer), *merge_casts])
    result, _ = residual_matmul(merged, w_out[None], 0, h, scale=1.0, gain=next_gain, name="out_proj")
    return result, proj_done + merge_done


def kernel(x, positions, ffn1_norm, ffn1_w13, ffn1_w2, mix_norm, w_in, b_forget, ret_norm, mla_q_norm,
           mla_kv_norm, w_uq, w_ukv, w_up_ret, w_up_fox, w_up_mla, w_gate, b_gate, w_out, ffn2_norm,
           ffn2_w13, ffn2_w2, final_norm):
    p = dict(mix_norm=mix_norm, w_in=w_in, b_forget=b_forget, ret_norm=ret_norm, mla_q_norm=mla_q_norm,
             mla_kv_norm=mla_kv_norm, w_uq=w_uq, w_ukv=w_ukv, w_up_ret=w_up_ret, w_up_fox=w_up_fox,
             w_up_mla=w_up_mla, w_gate=w_gate, b_gate=b_gate, w_out=w_out, ffn1_w13=ffn1_w13,
             ffn1_w2=ffn1_w2, ffn2_w13=ffn2_w13, ffn2_w2=ffn2_w2)
    batch, s, d = x.shape
    h = x.reshape(batch * s, d)
    w = _prepare_weights(p)
    tables, (w13_first,) = rope_tables(positions, casts=[(ffn1_w13, 0)])
    depth = ffn1_norm.shape[0]
    u, ssq = prenorm(h, ffn1_norm[0])
    w_gate_rows = w_gate.reshape(depth, 3 * d, d)
    w13, w13_layer = w13_first[None], 0
    for layer in range(depth):
        last = layer + 1 == depth
        (h, u, ssq), (gate, w_in_b) = _ffn(h, u, ssq, w13, w13_layer, ffn1_w2, layer, mix_norm[layer],
                                           [(w_gate_rows, layer), w["w_in_cast"](layer)])
        (h, u, ssq), (w2_b, w13_b) = _mixer(h, u, ssq, batch, tables, p, w, w_in_b, gate.reshape(3, d, d), layer,
                                            ffn2_norm[layer], [(ffn2_w2, layer)], [(ffn2_w13, layer)])
        g, next_w13 = swiglu_up(u, ssq, w13_b[None], 0, casts=[] if last else [(ffn1_w13, layer + 1)])
        result, _ = residual_matmul(g, w2_b[None], 0, h, scale=0.5, gain=None if last else ffn1_norm[layer + 1],
                                    bm=512, bn=512, name="ffn_down")
        if last:
            h = result
        else:
            h, u, ssq = result
            w13, w13_layer = next_w13[0][None], 0
    return rmsnorm(h, final_norm, x.dtype).reshape(batch, s, d)
```
